```python
import jax, jax.numpy as jnp
from jax import lax
import numpy as np

D_MODEL = 4096
BATCH = 1
SEQ = 8192
DEPTH = 1
DEC_BATCH = 32
DEC_SEQ = 1
PAST_LEN = 8192
PAGE_SIZE = 128

N_HEADS = 16
D_HEAD = 128
ATT_WIDTH = N_HEADS * D_HEAD
CONV_WIDTH = D_MODEL // 2
CONV_K = 31
Q_BLOCK = 128
N_GROUPS = 8
EXPERTS_PER_GROUP = 8
N_EXPERTS = N_GROUPS * EXPERTS_PER_GROUP
TOP_K = 2
D_EXPERT = D_MODEL // 4
MOE_BLOCK = 128
LN_EPS = 1e-5
ALPHA = (2.0 * DEPTH) ** 0.25
BETA = (8.0 * DEPTH) ** -0.25
SB_BIAS_INIT = -7.0
SPLITS = (ATT_WIDTH, 2 * ATT_WIDTH, 3 * ATT_WIDTH, 3 * ATT_WIDTH + 2 * CONV_WIDTH)

kernel_name = "stickbreak_conformer_hiermoe_decoder_step"


def _layer_norm(x, g, b):
    xf = x.astype(jnp.float32)
    mu = jnp.mean(xf, axis=-1, keepdims=True)
    var = jnp.mean(jnp.square(xf - mu), axis=-1, keepdims=True)
    y = (xf - mu) * lax.rsqrt(var + LN_EPS) * g.astype(jnp.float32) + b.astype(jnp.float32)
    return y.astype(x.dtype)


def _stick_breaking(q, k, v, bias, q_pos0):
    b, tq, h, dh = q.shape
    tk = k.shape[1]
    blk = Q_BLOCK if tq % Q_BLOCK == 0 else tq
    nblk = tq // blk
    scale = dh ** -0.5
    kf = k.astype(jnp.float32)
    vf = v.astype(jnp.float32)
    bf = bias.astype(jnp.float32)[None, :, None, None]
    k_pos = jnp.arange(tk)
    qb = q.reshape(b, nblk, blk, h, dh).transpose(1, 0, 2, 3, 4)

    def one_block(args):
        q_blk, i = args
        z = jnp.einsum('bqhd,bkhd->bhqk', q_blk.astype(jnp.float32), kf) * scale + bf
        q_pos = q_pos0 + i * blk + jnp.arange(blk)
        mask = k_pos[None, :] < q_pos[:, None]
        log_keep = jnp.where(mask, jax.nn.log_sigmoid(-z), 0.0)
        later = lax.cumsum(log_keep, axis=3, reverse=True) - log_keep
        w = jnp.where(mask, jnp.exp(jax.nn.log_sigmoid(z) + later), 0.0)
        return jnp.einsum('bhqk,bkhd->bqhd', w, vf)

    out = lax.map(one_block, (qb, jnp.arange(nblk)))
    return out.transpose(1, 0, 2, 3, 4).reshape(b, tq, h, dh).astype(q.dtype)


def _prompt_attention(q, k, v, bias):
    return _stick_breaking(q, k, v, bias, 0)


def _sample_attention(q, k_new, v_new, bias, cache_k, cache_v, page_table, l):
    past_len = page_table.shape[1] * PAGE_SIZE

    def one_seq(args):
        q_b, kn_b, vn_b, pages = args
        k_b = jnp.concatenate([cache_k[l, pages].reshape(past_len, N_HEADS, D_HEAD).astype(kn_b.dtype), kn_b], axis=0)
        v_b = jnp.concatenate([cache_v[l, pages].reshape(past_len, N_HEADS, D_HEAD).astype(vn_b.dtype), vn_b], axis=0)
        return _stick_breaking(q_b[None], k_b[None], v_b[None], bias, past_len)[0]

    return lax.map(one_seq, (q, k_new, v_new, page_table))


def _conv_branch(glu_in, hist, conv_w, conv_b, ln_g, ln_b, w_proj):
    val, gate = jnp.split(glu_in, 2, axis=-1)
    u = val * jax.nn.sigmoid(gate)
    u_full = jnp.concatenate([hist.astype(u.dtype), u], axis=1)
    c = lax.conv_general_dilated(u_full, conv_w[:, None, :].astype(u.dtype), (1,), 'VALID',
                                 dimension_numbers=('NWC', 'WIO', 'NWC'),
                                 feature_group_count=CONV_WIDTH)
    c = jax.nn.silu(_layer_norm(c + conv_b, ln_g, ln_b))
    return c @ w_proj, u_full[:, u_full.shape[1] - (CONV_K - 1):]


def _hier_moe(x, l, w_rg, b_rg, w_re, b_re, w1, w3, w2):
    n = x.shape[0]
    xf = x.astype(jnp.float32)
    g_prob = jax.nn.softmax(xf @ w_rg.astype(jnp.float32) + b_rg.astype(jnp.float32), axis=-1)
    g_p, g_idx = lax.top_k(g_prob, 1)
    e_logits = (xf @ w_re.astype(jnp.float32) + b_re.astype(jnp.float32)).reshape(n, N_GROUPS, EXPERTS_PER_GROUP)
    sel = jnp.broadcast_to(g_idx[:, :, None], (n, 1, EXPERTS_PER_GROUP))
    e_logits = jnp.take_along_axis(e_logits, sel, axis=1)[:, 0]
    top_logit, top_local = lax.top_k(e_logits, TOP_K)
    gate = g_p * jax.nn.softmax(top_logit, axis=-1)
    expert = g_idx * EXPERTS_PER_GROUP + top_local

    n_assign = n * TOP_K
    blk = max(1, min(MOE_BLOCK, n_assign // N_EXPERTS))
    n_blocks = -(-(n_assign + N_EXPERTS * (blk - 1)) // blk)
    n_rows = n_blocks * blk
    flat_e = expert.reshape(-1)
    flat_g = gate.reshape(-1)
    order = jnp.argsort(flat_e)
    sorted_e = flat_e[order]
    counts = jnp.bincount(flat_e, length=N_EXPERTS)
    padded = (counts + blk - 1) // blk * blk
    start = jnp.cumsum(counts) - counts
    pad_end = jnp.cumsum(padded)
    pad_start = pad_end - padded
    dest = pad_start[sorted_e] + (jnp.arange(n_assign) - start[sorted_e])
    row_tok = jnp.full((n_rows,), n, jnp.int32).at[dest].set((order // TOP_K).astype(jnp.int32))
    row_gate = jnp.zeros((n_rows,), jnp.float32).at[dest].set(flat_g[order])
    block_e = jnp.minimum(jnp.searchsorted(pad_end, jnp.arange(n_blocks) * blk, side='right'), N_EXPERTS - 1)
    x_pad = jnp.concatenate([x, jnp.zeros((1, x.shape[1]), x.dtype)], axis=0)

    def expert_block(args):
        toks, e = args
        xb = x_pad[toks]
        hdn = jax.nn.silu(xb @ w1[l, e]) * (xb @ w3[l, e])
        return hdn @ w2[l, e]

    y_rows = lax.map(expert_block, (row_tok.reshape(n_blocks, blk), block_e))
    y_rows = y_rows.reshape(n_rows, -1) * row_gate[:, None].astype(y_rows.dtype)
    return jnp.zeros_like(x_pad).at[row_tok].add(y_rows.astype(x.dtype))[:n]


def _layer(x, attn_fn, conv_hist, l, w_in, b_sb, b_gate, conv_w, conv_b, conv_ln_g, conv_ln_b,
           w_att_out, w_conv_out, w_out, ln1_g, ln1_b, w_router_group, b_router_group,
           w_router_expert, b_router_expert, w1, w3, w2, ln2_g, ln2_b):
    bsz, t, _ = x.shape
    proj = x @ w_in[l]
    q, k, v, glu_in, gate_logits = jnp.split(proj, SPLITS, axis=-1)
    q = q.reshape(bsz, t, N_HEADS, D_HEAD)
    k = k.reshape(bsz, t, N_HEADS, D_HEAD)
    v = v.reshape(bsz, t, N_HEADS, D_HEAD)
    att = attn_fn(q, k, v, b_sb[l]).reshape(bsz, t, ATT_WIDTH) @ w_att_out[l]
    cnv, conv_state = _conv_branch(glu_in, conv_hist, conv_w[l], conv_b[l], conv_ln_g[l], conv_ln_b[l], w_conv_out[l])
    g_att, g_cnv = jnp.split(jax.nn.sigmoid(gate_logits + b_gate[l]), 2, axis=-1)
    mixed = (g_att * att + g_cnv * cnv) @ w_out[l]
    x1 = _layer_norm(ALPHA * x + mixed, ln1_g[l], ln1_b[l])
    ffn = _hier_moe(x1.reshape(bsz * t, D_MODEL), l, w_router_group[l], b_router_group[l],
                    w_router_expert[l], b_router_expert[l], w1, w3, w2).reshape(bsz, t, D_MODEL)
    y = _layer_norm(ALPHA * x1 + ffn, ln2_g[l], ln2_b[l])
    return y, k, v, conv_state


def setup_inputs(seed: int = 0) -> dict:
    key = jax.random.key(seed)
    ks = jax.random.split(key, 32)
    n_pages = PAST_LEN // PAGE_SIZE
    n_used = DEC_BATCH * n_pages
    n_phys = n_used + (n_used + 3) // 4
    d, a, c = D_MODEL, ATT_WIDTH, CONV_WIDTH

    def nrm(k, shape, scale):
        return jax.random.normal(k, shape, jnp.float32) * scale

    x_prompt = nrm(ks[0], (BATCH, SEQ, d), 1.0)
    x_sample = nrm(ks[1], (DEC_BATCH, DEC_SEQ, d), 1.0)
    cache_k = nrm(ks[2], (DEPTH, n_phys, PAGE_SIZE, N_HEADS, D_HEAD), 1.0)
    cache_v = nrm(ks[3], (DEPTH, n_phys, PAGE_SIZE, N_HEADS, D_HEAD), BETA)
    state_conv = nrm(ks[4], (DEPTH, DEC_BATCH, CONV_K - 1, c), 0.5)
    page_table = jax.random.permutation(ks[5], n_phys)[:n_used].reshape(DEC_BATCH, n_pages).astype(jnp.int32)
    w_in = jnp.concatenate([
        nrm(ks[6], (DEPTH, d, 2 * a), d ** -0.5),
        nrm(ks[7], (DEPTH, d, a), BETA * d ** -0.5),
        nrm(ks[8], (DEPTH, d, 2 * c), d ** -0.5),
        nrm(ks[9], (DEPTH, d, 2 * d), d ** -0.5),
    ], axis=-1)
    b_sb = SB_BIAS_INIT + nrm(ks[29], (DEPTH, N_HEADS), 0.1)
    b_gate = nrm(ks[10], (DEPTH, 2 * d), 0.02)
    conv_w = nrm(ks[11], (DEPTH, CONV_K, c), CONV_K ** -0.5)
    conv_b = nrm(ks[12], (DEPTH, c), 0.01)
    conv_ln_g = 1.0 + nrm(ks[13], (DEPTH, c), 0.02)
    conv_ln_b = nrm(ks[14], (DEPTH, c), 0.02)
    w_att_out = nrm(ks[15], (DEPTH, a, d), BETA * a ** -0.5)
    w_conv_out = nrm(ks[16], (DEPTH, c, d), BETA * c ** -0.5)
    w_out = nrm(ks[17], (DEPTH, d, d), BETA * d ** -0.5)
    ln1_g = 1.0 + nrm(ks[18], (DEPTH, d), 0.02)
    ln1_b = nrm(ks[19], (DEPTH, d), 0.02)
    w_router_group = nrm(ks[20], (DEPTH, d, N_GROUPS), d ** -0.5)
    b_router_group = nrm(ks[21], (DEPTH, N_GROUPS), 0.01)
    w_router_expert = nrm(ks[22], (DEPTH, d, N_EXPERTS), d ** -0.5)
    b_router_expert = nrm(ks[23], (DEPTH, N_EXPERTS), 0.01)
    w1 = nrm(ks[24], (DEPTH, N_EXPERTS, d, D_EXPERT), d ** -0.5)
    w3 = nrm(ks[25], (DEPTH, N_EXPERTS, d, D_EXPERT), BETA * d ** -0.5)
    w2 = nrm(ks[26], (DEPTH, N_EXPERTS, D_EXPERT, d), BETA * D_EXPERT ** -0.5)
    ln2_g = 1.0 + nrm(ks[27], (DEPTH, d), 0.02)
    ln2_b = nrm(ks[28], (DEPTH, d), 0.02)
    return {"x_prompt": x_prompt, "x_sample": x_sample, "cache_k": cache_k, "cache_v": cache_v,
            "state_conv": state_conv, "page_table": page_table, "w_in": w_in, "b_sb": b_sb,
            "b_gate": b_gate, "conv_w": conv_w, "conv_b": conv_b, "conv_ln_g": conv_ln_g,
            "conv_ln_b": conv_ln_b, "w_att_out": w_att_out, "w_conv_out": w_conv_out, "w_out": w_out,
            "ln1_g": ln1_g, "ln1_b": ln1_b, "w_router_group": w_router_group,
            "b_router_group": b_router_group, "w_router_expert": w_router_expert,
            "b_router_expert": b_router_expert, "w1": w1, "w3": w3, "w2": w2,
            "ln2_g": ln2_g, "ln2_b": ln2_b}


def reference(x_prompt, x_sample, cache_k, cache_v, state_conv, page_table, w_in, b_sb, b_gate,
              conv_w, conv_b, conv_ln_g, conv_ln_b, w_att_out, w_conv_out, w_out, ln1_g, ln1_b,
              w_router_group, b_router_group, w_router_expert, b_router_expert, w1, w3, w2,
              ln2_g, ln2_b):
    y_p, y_s = x_prompt, x_sample
    kp, vp, cp, ksm, vsm, csm = [], [], [], [], [], []
    for l in range(DEPTH):
        hist0 = jnp.zeros((y_p.shape[0], CONV_K - 1, CONV_WIDTH), y_p.dtype)
        y_p, k_l, v_l, c_l = _layer(y_p, _prompt_attention, hist0, l, w_in, b_sb, b_gate, conv_w, conv_b,
                                    conv_ln_g, conv_ln_b, w_att_out, w_conv_out, w_out, ln1_g, ln1_b,
                                    w_router_group, b_router_group, w_router_expert, b_router_expert,
                                    w1, w3, w2, ln2_g, ln2_b)
        kp.append(k_l); vp.append(v_l); cp.append(c_l)

        def sample_attn(q, k, v, bias, l=l):
            return _sample_attention(q, k, v, bias, cache_k, cache_v, page_table, l)

        y_s, k_l, v_l, c_l = _layer(y_s, sample_attn, state_conv[l], l, w_in, b_sb, b_gate, conv_w, conv_b,
                                    conv_ln_g, conv_ln_b, w_att_out, w_conv_out, w_out, ln1_g, ln1_b,
                                    w_router_group, b_router_group, w_router_expert, b_router_expert,
                                    w1, w3, w2, ln2_g, ln2_b)
        ksm.append(k_l); vsm.append(v_l); csm.append(c_l)
    k_prompt, v_prompt, conv_prompt = jnp.stack(kp), jnp.stack(vp), jnp.stack(cp)
    k_sample, v_sample, conv_sample = jnp.stack(ksm), jnp.stack(vsm), jnp.stack(csm)
    return (y_p, y_s, k_prompt, v_prompt, conv_prompt, k_sample, v_sample, conv_sample)
```

```python
import functools

import jax
import jax.numpy as jnp
from jax import lax
from jax.experimental import pallas as pl
from jax.experimental.pallas import tpu as pltpu

_F32 = jnp.float32
_BF16 = jnp.bfloat16

LN_EPS = 1e-5
TOP_K = 2
_VMEM_LIMIT_V7X = 56 * 1024 * 1024
_LANES = 128
_MOE_ROWS = 128


def _params(*sem):
    return pltpu.CompilerParams(dimension_semantics=sem, vmem_limit_bytes=_VMEM_LIMIT_V7X)


def _blk(dim, pref):
    if dim <= pref:
        return dim
    assert dim % pref == 0, (dim, pref)
    return pref


def _split_bf16(x):
    hi = x.astype(_BF16)
    lo = (x - hi.astype(_F32)).astype(_BF16)
    return hi, lo


def _dot(x, w):
    assert x.dtype == w.dtype, (x.dtype, w.dtype)
    d = functools.partial(jnp.dot, preferred_element_type=_F32)
    if x.dtype == _BF16:
        return d(x, w)
    xh, xl = _split_bf16(x)
    wh, wl = _split_bf16(w)
    return d(xh, wh) + (d(xl, wh) + d(xh, wl))


def _sigmoid(x):
    return 1.0 / (1.0 + jnp.exp(-x))


def _layer_norm(v, g, b):
    mu = jnp.mean(v, axis=-1, keepdims=True)
    d = v - mu
    var = jnp.mean(d * d, axis=-1, keepdims=True)
    return d * lax.rsqrt(var + LN_EPS) * g + b


def _mm_kernel(x_ref, w_ref, o_ref, *, scale):
    acc = _dot(x_ref[...], w_ref[...])
    if scale != 1.0:
        acc = acc * scale
    o_ref[...] = acc.astype(o_ref.dtype)


def _mm2_kernel(x_ref, w_ref, o_ref, ob_ref):
    acc = _dot(x_ref[...], w_ref[...])
    o_ref[...] = acc
    ob_ref[...] = acc.astype(ob_ref.dtype)


def _glu_kernel(x_ref, wv_ref, wg_ref, o_ref):
    x = x_ref[...]
    o_ref[...] = _dot(x, wv_ref[...]) * _sigmoid(_dot(x, wg_ref[...]))


def _gate_kernel(x_ref, w_ref, b_ref, o_ref):
    o_ref[...] = _sigmoid(_dot(x_ref[...], w_ref[...]) + b_ref[...])


def _mix_kernel(a_ref, c_ref, wa_ref, wc_ref, ga_ref, gc_ref, o_ref):
    att = _dot(a_ref[...], wa_ref[...])
    cnv = _dot(c_ref[...], wc_ref[...])
    o_ref[...] = (ga_ref[...] * att + gc_ref[...] * cnv).astype(o_ref.dtype)


def _row_spec(bm, k):
    return pl.BlockSpec((bm, k), lambda j, i: (i, 0))


def _col_spec(k, bn, off_blocks=0):
    return pl.BlockSpec((k, bn), lambda j, i: (0, j + off_blocks))


def _tile_spec(bm, bn, off_blocks=0):
    return pl.BlockSpec((bm, bn), lambda j, i: (i, j + off_blocks))


def _dense_call(kernel, name, m, n, bm, bn, in_specs, out_dtypes, args):
    outs = [jax.ShapeDtypeStruct((m, n), dt) for dt in out_dtypes]
    specs = [_tile_spec(bm, bn) for _ in out_dtypes]
    single = len(outs) == 1
    return pl.pallas_call(
        kernel, grid=(n // bn, m // bm), in_specs=in_specs,
        out_specs=specs[0] if single else specs,
        out_shape=outs[0] if single else outs,
        compiler_params=_params("parallel", "parallel"), name=name)(*args)


def _dense_blocks(x, n):
    m, k = x.shape
    prec3 = x.dtype == _F32
    bm = _blk(m, 512)
    bn = _blk(n, 256 if prec3 else 1024)
    return m, k, bm, bn


def _project(x, w, col0, n, out_dtype, scale=1.0, name="proj"):
    m, k, bm, bn = _dense_blocks(x, n)
    return _dense_call(functools.partial(_mm_kernel, scale=scale), name, m, n, bm, bn,
                       [_row_spec(bm, k), _col_spec(k, bn, col0 // bn)], [out_dtype], (x, w))


def _project_kv(x, w, col0, n, name):
    m, k, bm, bn = _dense_blocks(x, n)
    return _dense_call(_mm2_kernel, name, m, n, bm, bn,
                       [_row_spec(bm, k), _col_spec(k, bn, col0 // bn)], [_F32, _BF16], (x, w))


def _project_glu(x, w, col0, c, name="glu"):
    m, k, bm, bn = _dense_blocks(x, c)
    return _dense_call(_glu_kernel, name, m, c, bm, bn,
                       [_row_spec(bm, k), _col_spec(k, bn, col0 // bn), _col_spec(k, bn, (col0 + c) // bn)],
                       [_F32], (x, w, w))


def _project_gate(x, w, col0, n, b_gate, name="gate"):
    m, k, bm, bn = _dense_blocks(x, n)
    b_spec = pl.BlockSpec((1, bn), lambda j, i: (0, j))
    return _dense_call(_gate_kernel, name, m, n, bm, bn,
                       [_row_spec(bm, k), _col_spec(k, bn, col0 // bn), b_spec], [_F32],
                       (x, w, b_gate.reshape(1, n)))


def _mix(att, cnv, w_att, w_cnv, gates, out_dtype, name="mix"):
    n = w_att.shape[1]
    m, ka, bm, bn = _dense_blocks(att, n)
    kc = cnv.shape[1]
    return _dense_call(_mix_kernel, name, m, n, bm, bn,
                       [_row_spec(bm, ka), _row_spec(bm, kc), _col_spec(ka, bn), _col_spec(kc, bn),
                        _tile_spec(bm, bn), _tile_spec(bm, bn, n // bn)],
                       [out_dtype], (att, cnv, w_att, w_cnv, gates, gates))


def _res_ln_kernel(x_ref, r_ref, g_ref, b_ref, o_ref, *, alpha):
    o_ref[...] = _layer_norm(alpha * x_ref[...] + r_ref[...], g_ref[...], b_ref[...])


def _res_ln(x, r, g, b, alpha, name="res_ln"):
    m, d = x.shape
    bt = _blk(m, 256)
    row = pl.BlockSpec((bt, d), lambda i: (i, 0))
    vec = pl.BlockSpec((1, d), lambda i: (0, 0))
    return pl.pallas_call(
        functools.partial(_res_ln_kernel, alpha=alpha), grid=(m // bt,),
        in_specs=[row, row, vec, vec], out_specs=row,
        out_shape=jax.ShapeDtypeStruct((m, d), _F32),
        compiler_params=_params("parallel"), name=name)(x, r, g.reshape(1, d), b.reshape(1, d))


def _router_kernel(x_ref, w_ref, b_ref, o_ref, *, n_groups, per_group):
    logits = _dot(x_ref[...], w_ref[...]) + b_ref[...]
    lane = lax.broadcasted_iota(jnp.int32, logits.shape, 1).astype(_F32)
    neg = -jnp.inf
    big = 1e9

    def arg_max(v):
        m = jnp.max(v, axis=-1, keepdims=True)
        return m, jnp.min(jnp.where(v == m, lane, big), axis=-1, keepdims=True)

    glog = jnp.where(lane < n_groups, logits, neg)
    gmax, g_idx = arg_max(glog)
    g_p = 1.0 / jnp.sum(jnp.exp(glog - gmax), axis=-1, keepdims=True)
    lo = n_groups + g_idx * per_group
    elog = jnp.where((lane >= lo) & (lane < lo + per_group), logits, neg)
    m1, i1 = arg_max(elog)
    m2, i2 = arg_max(jnp.where(lane == i1, neg, elog))
    e2 = jnp.exp(m2 - m1)
    p1 = 1.0 / (1.0 + e2)
    p2 = e2 / (1.0 + e2)
    o_ref[...] = jnp.where(lane == 0, i1 - n_groups,
                 jnp.where(lane == 1, i2 - n_groups,
                 jnp.where(lane == 2, g_p * p1,
                 jnp.where(lane == 3, g_p * p2, 0.0))))


def _router(x1, w_r, b_r, n_groups, per_group, name="router"):
    m, d = x1.shape
    bt = _blk(m, 256)
    return pl.pallas_call(
        functools.partial(_router_kernel, n_groups=n_groups, per_group=per_group), grid=(m // bt,),
        in_specs=[pl.BlockSpec((bt, d), lambda i: (i, 0)),
                  pl.BlockSpec((d, _LANES), lambda i: (0, 0)),
                  pl.BlockSpec((1, _LANES), lambda i: (0, 0))],
        out_specs=pl.BlockSpec((bt, _LANES), lambda i: (i, 0)),
        out_shape=jax.ShapeDtypeStruct((m, _LANES), _F32),
        compiler_params=_params("parallel"), name=name)(x1, w_r, b_r)


def _stick_terms(z):
    soft = jnp.log(1.0 + jnp.exp(-jnp.abs(z)))
    return -(jnp.maximum(z, 0.0) + soft), jnp.minimum(z, 0.0) - soft


def _attn_prompt_kernel(bias_ref, q_ref, k_ref, v_ref, o_ref, acc_ref, *, bq, bk):
    h = pl.program_id(0)
    qi = pl.program_id(1)
    bias = bias_ref[h]
    q = q_ref[...]
    r = lax.broadcasted_iota(jnp.int32, (bk, bk), 0)
    c = lax.broadcasted_iota(jnp.int32, (bk, bk), 1)
    later_mat = jnp.where(r > c, 1.0, 0.0).astype(_BF16)
    q_pos = qi * bq + lax.broadcasted_iota(jnp.int32, (bq, bk), 0)
    k_off = lax.broadcasted_iota(jnp.int32, (bq, bk), 1)
    acc_ref[...] = jnp.zeros_like(acc_ref)

    def step(j, carry, masked):
        k0 = pl.multiple_of(j * bk, bk)
        kb = k_ref[pl.ds(k0, bk), :]
        vb = v_ref[pl.ds(k0, bk), :]
        z = lax.dot_general(q, kb, (((1,), (1,)), ((), ())), preferred_element_type=_F32) + bias
        log_keep, log_beta = _stick_terms(z)
        if masked:
            mask = (k0 + k_off) < q_pos
            log_keep = jnp.where(mask, log_keep, 0.0)
        hi, lo = _split_bf16(log_keep)
        within = jnp.dot(hi, later_mat, preferred_element_type=_F32) + jnp.dot(lo, later_mat, preferred_element_type=_F32)
        w = jnp.exp(log_beta + within + carry)
        if masked:
            w = jnp.where(mask, w, 0.0)
        acc_ref[...] += jnp.dot(w.astype(_BF16), vb, preferred_element_type=_F32)
        return carry + jnp.sum(log_keep, axis=-1, keepdims=True)

    nd = bq // bk
    carry = jnp.zeros((bq, 1), _F32)
    for d in range(nd):
        carry = step((qi + 1) * nd - 1 - d, carry, True)
    n_full = qi * nd
    lax.fori_loop(0, n_full, lambda t, cr: step(n_full - 1 - t, cr, False), carry)
    o_ref[...] = acc_ref[...].astype(o_ref.dtype)


def _attn_prompt(q, k, v, bias, n_heads, name="attn_prompt"):
    t, a = q.shape
    dh = a // n_heads
    bq = _blk(t, 512)
    bk = _blk(bq, 256)
    kernel = functools.partial(_attn_prompt_kernel, bq=bq, bk=bk)
    return pl.pallas_call(
        kernel, grid=(n_heads, t // bq),
        in_specs=[pl.BlockSpec(memory_space=pltpu.SMEM),
                  pl.BlockSpec((bq, dh), lambda h, i: (i, h)),
                  pl.BlockSpec((t, dh), lambda h, i: (0, h)),
                  pl.BlockSpec((t, dh), lambda h, i: (0, h))],
        out_specs=pl.BlockSpec((bq, dh), lambda h, i: (i, h)),
        out_shape=jax.ShapeDtypeStruct((t, a), _BF16),
        scratch_shapes=[pltpu.VMEM((bq, dh), _F32)],
        compiler_params=_params("parallel", "parallel"), name=name)(bias, q, k, v)


def _attn_sample_kernel(pt_ref, q_ref, bias_ref, k_ref, v_ref, o_ref, acc_ref, carry_ref, *, n_heads, n_pages):
    p = pl.program_id(1)
    hh = n_heads
    cols = k_ref.shape[0]
    nblk = cols // _LANES

    @pl.when(p == 0)
    def _():
        acc_ref[...] = jnp.zeros_like(acc_ref)
        carry_ref[...] = jnp.zeros_like(carry_ref)

    q = q_ref[...].astype(_BF16)
    kp = k_ref[...].astype(_BF16)
    vp = v_ref[...].astype(_BF16)
    z = lax.dot_general(q, kp, (((1,), (1,)), ((), ())), preferred_element_type=_F32) + bias_ref[...]
    log_keep, log_beta = _stick_terms(z)

    r = lax.broadcasted_iota(jnp.int32, (_LANES, _LANES), 0)
    c = lax.broadcasted_iota(jnp.int32, (_LANES, _LANES), 1)
    same = (r & (hh - 1)) == (c & (hh - 1))
    later_mat = jnp.where(same & (r > c), 1.0, 0.0).astype(_BF16)
    total_mat = jnp.where(same, 1.0, 0.0).astype(_BF16)
    stacked = jnp.concatenate([log_keep[:, i * _LANES:(i + 1) * _LANES] for i in range(nblk)], axis=0)
    hi, lo = _split_bf16(stacked)
    d = functools.partial(jnp.dot, preferred_element_type=_F32)
    within = d(hi, later_mat) + d(lo, later_mat)
    totals = d(hi, total_mat) + d(lo, total_mat)
    suffix = carry_ref[...]
    later = [None] * nblk
    for i in range(nblk - 1, -1, -1):
        later[i] = within[i * hh:(i + 1) * hh, :] + suffix
        suffix = suffix + totals[i * hh:(i + 1) * hh, :]
    carry_ref[...] = suffix
    w = jnp.exp(log_beta + jnp.concatenate(later, axis=1))
    row = lax.broadcasted_iota(jnp.int32, w.shape, 0)
    col = lax.broadcasted_iota(jnp.int32, w.shape, 1)
    w = jnp.where((col & (hh - 1)) == row, w, 0.0).astype(_BF16)
    acc_ref[...] += jnp.dot(w, vp, preferred_element_type=_F32)

    @pl.when(p == n_pages - 1)
    def _():
        o_ref[...] = acc_ref[...]


def _attn_sample(q, cache_k, cache_v, page_table, bias, layer, name="attn_sample"):
    b, hh, dh = q.shape
    assert hh & (hh - 1) == 0 and _LANES % hh == 0
    depth, n_phys, page = cache_k.shape[:3]
    n_pages = page_table.shape[1]
    ck = cache_k.reshape(depth, n_phys, page * hh, dh)
    cv = cache_v.reshape(depth, n_phys, page * hh, dh)
    page_spec = pl.BlockSpec((None, None, page * hh, dh),
                             lambda i, p, pt: (layer, pt[i * n_pages + (n_pages - 1 - p)], 0, 0))
    grid_spec = pltpu.PrefetchScalarGridSpec(
        num_scalar_prefetch=1, grid=(b, n_pages),
        in_specs=[pl.BlockSpec((None, hh, dh), lambda i, p, pt: (i, 0, 0)),
                  pl.BlockSpec((hh, 1), lambda i, p, pt: (0, 0)),
                  page_spec, page_spec],
        out_specs=pl.BlockSpec((None, hh, dh), lambda i, p, pt: (i, 0, 0)),
        scratch_shapes=[pltpu.VMEM((hh, dh), _F32), pltpu.VMEM((hh, _LANES), _F32)])
    return pl.pallas_call(
        functools.partial(_attn_sample_kernel, n_heads=hh, n_pages=n_pages),
        grid_spec=grid_spec, out_shape=jax.ShapeDtypeStruct((b, hh, dh), _F32),
        compiler_params=_params("parallel", "arbitrary"), name=name)(
            page_table.reshape(-1), q, bias.reshape(hh, 1), ck, cv)


_HALO = 32


def _conv_prompt_kernel(u_ref, prev_ref, hist_ref, w_ref, b_ref, g_ref, beta_ref, o_ref, win_ref, y_ref,
                        *, taps, tb, cw):
    i = pl.program_id(0)
    bt, ch = u_ref.shape
    lead = _HALO - (taps - 1)
    win_ref[0:_HALO, :] = jnp.where(i == 0, hist_ref[...], prev_ref[...])
    win_ref[_HALO:, :] = u_ref[...]
    for c0 in range(0, ch, cw):
        wk = w_ref[:, c0:c0 + cw]
        for t0 in range(0, bt, tb):
            acc = jnp.zeros((tb, cw), _F32)
            for k in range(taps):
                acc = acc + win_ref[t0 + k + lead:t0 + k + lead + tb, c0:c0 + cw] * wk[k:k + 1, :]
            y_ref[t0:t0 + tb, c0:c0 + cw] = acc
    y = _layer_norm(y_ref[...] + b_ref[...], g_ref[...], beta_ref[...])
    o_ref[...] = (y * _sigmoid(y)).astype(o_ref.dtype)


def _conv_prompt(u, hist, conv_w, conv_b, ln_g, ln_b, out_dtype, name="conv_prompt"):
    t, ch = u.shape
    taps = conv_w.shape[0]
    assert taps - 1 <= _HALO and t % _HALO == 0
    bt = _blk(t, 128)
    hist_pad = jnp.concatenate([jnp.zeros((_HALO - (taps - 1), ch), u.dtype), hist.astype(u.dtype)], axis=0)
    ratio = bt // _HALO
    vec = pl.BlockSpec((1, ch), lambda i: (0, 0))
    kernel = functools.partial(_conv_prompt_kernel, taps=taps, tb=min(bt, 32), cw=min(ch, 512))
    return pl.pallas_call(
        kernel, grid=(t // bt,),
        in_specs=[pl.BlockSpec((bt, ch), lambda i: (i, 0)),
                  pl.BlockSpec((_HALO, ch), lambda i: (jnp.maximum(i * ratio - 1, 0), 0)),
                  pl.BlockSpec((_HALO, ch), lambda i: (0, 0)),
                  pl.BlockSpec((taps, ch), lambda i: (0, 0)), vec, vec, vec],
        out_specs=pl.BlockSpec((bt, ch), lambda i: (i, 0)),
        out_shape=jax.ShapeDtypeStruct((t, ch), out_dtype),
        scratch_shapes=[pltpu.VMEM((bt + _HALO, ch), _F32), pltpu.VMEM((bt, ch), _F32)],
        compiler_params=_params("parallel"), name=name)(
            u, u, hist_pad, conv_w, conv_b.reshape(1, ch), ln_g.reshape(1, ch), ln_b.reshape(1, ch))


def _conv_sample_kernel(u_ref, s_ref, w_ref, b_ref, g_ref, beta_ref, o_ref, ns_ref, *, taps):
    s = s_ref[...]
    u = u_ref[...]
    conv = jnp.sum(s * w_ref[0:taps - 1, :], axis=0, keepdims=True) + u * w_ref[taps - 1:taps, :]
    y = _layer_norm(conv + b_ref[...], g_ref[...], beta_ref[...])
    o_ref[...] = y * _sigmoid(y)
    ns_ref[0:taps - 2, :] = s_ref[1:taps - 1, :]
    ns_ref[taps - 2:taps - 1, :] = u


def _conv_sample(u, state, conv_w, conv_b, ln_g, ln_b, name="conv_sample"):
    b, ch = u.shape
    taps = conv_w.shape[0]
    vec = pl.BlockSpec((1, ch), lambda i: (0, 0))
    row = pl.BlockSpec((None, 1, ch), lambda i: (i, 0, 0))
    st = pl.BlockSpec((None, taps - 1, ch), lambda i: (i, 0, 0))
    out, new_state = pl.pallas_call(
        functools.partial(_conv_sample_kernel, taps=taps), grid=(b,),
        in_specs=[row, st, pl.BlockSpec((taps, ch), lambda i: (0, 0)), vec, vec, vec],
        out_specs=[row, st],
        out_shape=[jax.ShapeDtypeStruct((b, 1, ch), _F32), jax.ShapeDtypeStruct((b, taps - 1, ch), _F32)],
        compiler_params=_params("parallel"), name=name)(
            u.reshape(b, 1, ch), state, conv_w, conv_b.reshape(1, ch), ln_g.reshape(1, ch), ln_b.reshape(1, ch))
    return out.reshape(b, ch), new_state


def _gather_kernel(tok_ref, x_hbm, o_ref, buf_ref, sem, *, rows):
    i = pl.program_id(0)

    def issue(r, _):
        tok = tok_ref[i * rows + r]
        pltpu.make_async_copy(x_hbm.at[pl.ds(tok, 1), :], buf_ref.at[pl.ds(r, 1), :], sem).start()
        return 0

    lax.fori_loop(0, rows, issue, 0)

    def drain(r, _):
        pltpu.make_async_copy(x_hbm.at[pl.ds(0, 1), :], buf_ref.at[pl.ds(r, 1), :], sem).wait()
        return 0

    lax.fori_loop(0, rows, drain, 0)
    o_ref[...] = buf_ref[...].astype(o_ref.dtype)


def _gather_rows(x, row_tok, rows, name="moe_gather"):
    n_rows = row_tok.shape[0]
    d = x.shape[1]
    grid_spec = pltpu.PrefetchScalarGridSpec(
        num_scalar_prefetch=1, grid=(n_rows // rows,),
        in_specs=[pl.BlockSpec(memory_space=pl.ANY)],
        out_specs=pl.BlockSpec((rows, d), lambda i, tok: (i, 0)),
        scratch_shapes=[pltpu.VMEM((rows, d), x.dtype), pltpu.SemaphoreType.DMA(())])
    return pl.pallas_call(
        functools.partial(_gather_kernel, rows=rows), grid_spec=grid_spec,
        out_shape=jax.ShapeDtypeStruct((n_rows, d), _BF16),
        compiler_params=_params("arbitrary"), name=name)(row_tok, x)


def _expert_changed(be_ref, i):
    prev = be_ref[jnp.maximum(i - 1, 0)]
    return (i == 0) | (be_ref[i] != prev)


def _gmm1_kernel(be_ref, na_ref, x_ref, w1_ref, w3_ref, o_ref, w1b_ref, w3b_ref):
    i = pl.program_id(1)

    @pl.when(i < na_ref[0])
    def _():
        @pl.when(_expert_changed(be_ref, i))
        def _():
            w1b_ref[...] = w1_ref[...].astype(_BF16)
            w3b_ref[...] = w3_ref[...].astype(_BF16)

        x = x_ref[...]
        a = _dot(x, w1b_ref[...])
        o_ref[...] = (a * _sigmoid(a) * _dot(x, w3b_ref[...])).astype(o_ref.dtype)

    @pl.when(i >= na_ref[0])
    def _():
        o_ref[...] = jnp.zeros_like(o_ref)


def _gmm2_kernel(be_ref, na_ref, h_ref, w2_ref, o_ref, w2b_ref):
    i = pl.program_id(1)

    @pl.when(i < na_ref[0])
    def _():
        @pl.when(_expert_changed(be_ref, i))
        def _():
            w2b_ref[...] = w2_ref[...].astype(_BF16)

        o_ref[...] = _dot(h_ref[...], w2b_ref[...])

    @pl.when(i >= na_ref[0])
    def _():
        o_ref[...] = jnp.zeros_like(o_ref)


def _expert_ffn(xs, block_e, n_active, w1, w3, w2, layer):
    n_rows, d = xs.shape
    f = w1.shape[-1]
    bm = _MOE_ROWS
    tn1 = _blk(f, 256)
    tn2 = _blk(d, 1024)
    w_in_spec = pl.BlockSpec((None, None, d, tn1), lambda j, i, be, na: (layer, be[i], 0, j))
    hidden = pl.pallas_call(
        _gmm1_kernel,
        grid_spec=pltpu.PrefetchScalarGridSpec(
            num_scalar_prefetch=2, grid=(f // tn1, n_rows // bm),
            in_specs=[pl.BlockSpec((bm, d), lambda j, i, be, na: (i, 0)), w_in_spec, w_in_spec],
            out_specs=pl.BlockSpec((bm, tn1), lambda j, i, be, na: (i, j)),
            scratch_shapes=[pltpu.VMEM((d, tn1), _BF16), pltpu.VMEM((d, tn1), _BF16)]),
        out_shape=jax.ShapeDtypeStruct((n_rows, f), _BF16),
        compiler_params=_params("arbitrary", "arbitrary"), name="moe_up")(block_e, n_active, xs, w1, w3)
    return pl.pallas_call(
        _gmm2_kernel,
        grid_spec=pltpu.PrefetchScalarGridSpec(
            num_scalar_prefetch=2, grid=(d // tn2, n_rows // bm),
            in_specs=[pl.BlockSpec((bm, f), lambda j, i, be, na: (i, 0)),
                      pl.BlockSpec((None, None, f, tn2), lambda j, i, be, na: (layer, be[i], 0, j))],
            out_specs=pl.BlockSpec((bm, tn2), lambda j, i, be, na: (i, j)),
            scratch_shapes=[pltpu.VMEM((f, tn2), _BF16)]),
        out_shape=jax.ShapeDtypeStruct((n_rows, d), _F32),
        compiler_params=_params("arbitrary", "arbitrary"), name="moe_down")(block_e, n_active, hidden, w2)


def _combine_kernel(pos_ref, x_ref, r_ref, g_ref, b_ref, ys_hbm, o_ref, buf_ref, sem, *, bt, alpha):
    i = pl.program_id(0)

    def issue(r, _):
        for k in range(TOP_K):
            pos = pos_ref[(i * bt + r) * TOP_K + k]
            pltpu.make_async_copy(ys_hbm.at[pl.ds(pos, 1), :], buf_ref.at[k, pl.ds(r, 1), :], sem).start()
        return 0

    lax.fori_loop(0, bt, issue, 0)

    def drain(r, _):
        for k in range(TOP_K):
            pltpu.make_async_copy(ys_hbm.at[pl.ds(0, 1), :], buf_ref.at[k, pl.ds(r, 1), :], sem).wait()
        return 0

    lax.fori_loop(0, bt, drain, 0)
    slab = r_ref[...]
    ffn = buf_ref[0] * slab[:, 2:3] + buf_ref[1] * slab[:, 3:4]
    o_ref[...] = _layer_norm(alpha * x_ref[...] + ffn, g_ref[...], b_ref[...])


def _combine_ln(x1, slab, pos, ys, g, b, alpha, name="moe_combine"):
    m, d = x1.shape
    bt = _blk(m, 128)
    vec = pl.BlockSpec((1, d), lambda i, pos: (0, 0))
    grid_spec = pltpu.PrefetchScalarGridSpec(
        num_scalar_prefetch=1, grid=(m // bt,),
        in_specs=[pl.BlockSpec((bt, d), lambda i, pos: (i, 0)),
                  pl.BlockSpec((bt, _LANES), lambda i, pos: (i, 0)), vec, vec,
                  pl.BlockSpec(memory_space=pl.ANY)],
        out_specs=pl.BlockSpec((bt, d), lambda i, pos: (i, 0)),
        scratch_shapes=[pltpu.VMEM((TOP_K, bt, d), _F32), pltpu.SemaphoreType.DMA(())])
    return pl.pallas_call(
        functools.partial(_combine_kernel, bt=bt, alpha=alpha), grid_spec=grid_spec,
        out_shape=jax.ShapeDtypeStruct((m, d), _F32),
        compiler_params=_params("arbitrary"), name=name)(pos, x1, slab, g.reshape(1, d), b.reshape(1, d), ys)


def _dispatch_plan(expert, n_experts, bm):
    a = expert.shape[0]
    n_blocks = -(-(a + n_experts * (bm - 1)) // bm)
    n_rows = n_blocks * bm
    onehot = (expert[:, None] == jnp.arange(n_experts, dtype=jnp.int32)[None, :]).astype(jnp.int32)
    rank = jnp.take_along_axis(jnp.cumsum(onehot, axis=0) - onehot, expert[:, None], axis=1)[:, 0]
    counts = jnp.sum(onehot, axis=0)
    padded = (counts + bm - 1) // bm * bm
    pad_end = jnp.cumsum(padded)
    pad_start = pad_end - padded
    dest = (pad_start[expert] + rank).astype(jnp.int32)
    row_src = jnp.full((n_rows,), -1, jnp.int32).at[dest].set(jnp.arange(a, dtype=jnp.int32))
    n_active = (pad_end[-1] // bm).astype(jnp.int32)
    blk = jnp.minimum(jnp.arange(n_blocks, dtype=jnp.int32), jnp.maximum(n_active - 1, 0))
    block_e = jnp.minimum(jnp.searchsorted(pad_end, blk * bm, side="right"), n_experts - 1).astype(jnp.int32)
    return dest, row_src, block_e, n_active.reshape(1)


def _branches(x, w_in, w_att_out, w_conv_out, w_out, b_gate, attn_fn, conv_fn, widths, kv_bf16):
    a, c, d = widths
    scale = float(a // attn_fn.n_heads) ** -0.5
    act = x.dtype
    q = _project(x, w_in, 0, a, act, scale=scale, name="proj_q")
    if kv_bf16:
        k, k_b = _project_kv(x, w_in, a, a, "proj_k")
        v, v_b = _project_kv(x, w_in, 2 * a, a, "proj_v")
    else:
        k = k_b = _project(x, w_in, a, a, _F32, name="proj_k")
        v = v_b = _project(x, w_in, 2 * a, a, _F32, name="proj_v")
    u = _project_glu(x, w_in, 3 * a, c)
    gates = _project_gate(x, w_in, 3 * a + 2 * c, 2 * d, b_gate)
    att = attn_fn(q, k_b, v_b)
    cnv, conv_state = conv_fn(u)
    pre = _mix(att, cnv, w_att_out, w_conv_out, gates, act)
    mixed = _project(pre, w_out, 0, d, _F32, name="proj_out")
    return mixed, k, v, conv_state


class _PromptAttn:
    def __init__(self, bias, n_heads):
        self.bias, self.n_heads = bias, n_heads

    def __call__(self, q, k, v):
        return _attn_prompt(q, k, v, self.bias, self.n_heads)


class _SampleAttn:
    def __init__(self, bias, n_heads, cache_k, cache_v, page_table, layer):
        self.bias, self.n_heads = bias, n_heads
        self.cache_k, self.cache_v, self.page_table, self.layer = cache_k, cache_v, page_table, layer

    def __call__(self, q, k, v):
        b, a = q.shape
        out = _attn_sample(q.reshape(b, self.n_heads, a // self.n_heads), self.cache_k, self.cache_v,
                           self.page_table, self.bias, self.layer)
        return out.reshape(b, a)


def kernel(x_prompt, x_sample, cache_k, cache_v, state_conv, page_table, w_in, b_sb, b_gate, conv_w, conv_b,
           conv_ln_g, conv_ln_b, w_att_out, w_conv_out, w_out, ln1_g, ln1_b, w_router_group, b_router_group,
           w_router_expert, b_router_expert, w1, w3, w2, ln2_g, ln2_b):
    depth, d, _ = w_in.shape
    n_heads, d_head = cache_k.shape[3], cache_k.shape[4]
    a = n_heads * d_head
    c = conv_w.shape[2]
    taps = conv_w.shape[1]
    n_groups = w_router_group.shape[2]
    n_experts = w_router_expert.shape[2]
    per_group = n_experts // n_groups
    assert n_groups + n_experts <= _LANES
    alpha = (2.0 * depth) ** 0.25
    bsz, seq, _ = x_prompt.shape
    dec, dec_seq, _ = x_sample.shape
    assert bsz == 1 and dec_seq == 1
    n_p, n_s = bsz * seq, dec * dec_seq

    y_p = x_prompt.reshape(n_p, d)
    y_s = x_sample.reshape(n_s, d)
    outs = {k: [] for k in ("kp", "vp", "cp", "ks", "vs", "cs")}
    for l in range(depth):
        win, wao, wco, wo = w_in[l], w_att_out[l], w_conv_out[l], w_out[l]
        hist0 = jnp.zeros((taps - 1, c), _F32)
        conv_p = lambda u, l=l, hist0=hist0: (
            _conv_prompt(u, hist0, conv_w[l], conv_b[l], conv_ln_g[l], conv_ln_b[l], _BF16), u[seq - (taps - 1):])
        mixed_p, k_p, v_p, cs_p = _branches(
            y_p.astype(_BF16), win.astype(_BF16), wao.astype(_BF16), wco.astype(_BF16), wo.astype(_BF16),
            b_gate[l], _PromptAttn(b_sb[l], n_heads), conv_p, (a, c, d), True)
        conv_s = lambda u, l=l: _conv_sample(u, state_conv[l], conv_w[l], conv_b[l], conv_ln_g[l], conv_ln_b[l])
        mixed_s, k_s, v_s, cs_s = _branches(
            y_s, win, wao, wco, wo, b_gate[l],
            _SampleAttn(b_sb[l], n_heads, cache_k, cache_v, page_table, l), conv_s, (a, c, d), False)

        x1_p = _res_ln(y_p, mixed_p, ln1_g[l], ln1_b[l], alpha, name="ln1_prompt")
        x1_s = _res_ln(y_s, mixed_s, ln1_g[l], ln1_b[l], alpha, name="ln1_sample")

        w_r = jnp.zeros((d, _LANES), _F32).at[:, :n_groups].set(w_router_group[l]) \
                 .at[:, n_groups:n_groups + n_experts].set(w_router_expert[l])
        b_r = jnp.zeros((1, _LANES), _F32).at[0, :n_groups].set(b_router_group[l]) \
                 .at[0, n_groups:n_groups + n_experts].set(b_router_expert[l])
        slab_p = _router(x1_p, w_r, b_r, n_groups, per_group, name="router_prompt")
        slab_s = _router(x1_s, w_r, b_r, n_groups, per_group, name="router_sample")
        expert = jnp.concatenate([slab_p[:, :TOP_K], slab_s[:, :TOP_K]], axis=0).astype(jnp.int32).reshape(-1)
        dest, row_src, block_e, n_active = _dispatch_plan(expert, n_experts, _MOE_ROWS)
        row_tok = jnp.maximum(row_src, 0) // TOP_K
        x1_all = jnp.concatenate([x1_p, x1_s], axis=0)
        xs = _gather_rows(x1_all, row_tok, _MOE_ROWS)
        ys = _expert_ffn(xs, block_e, n_active, w1, w3, w2, l)
        y_p = _combine_ln(x1_p, slab_p, dest[:n_p * TOP_K], ys, ln2_g[l], ln2_b[l], alpha, name="combine_prompt")
        y_s = _combine_ln(x1_s, slab_s, dest[n_p * TOP_K:], ys, ln2_g[l], ln2_b[l], alpha, name="combine_sample")

        outs["kp"].append(k_p.reshape(bsz, seq, n_heads, d_head))
        outs["vp"].append(v_p.reshape(bsz, seq, n_heads, d_head))
        outs["cp"].append(cs_p.reshape(bsz, taps - 1, c))
        outs["ks"].append(k_s.reshape(dec, dec_seq, n_heads, d_head))
        outs["vs"].append(v_s.reshape(dec, dec_seq, n_heads, d_head))
        outs["cs"].append(cs_s)
    return (y_p.reshape(bsz, seq, d), y_s.reshape(dec, dec_seq, d),
            jnp.stack(outs["kp"]), jnp.stack(outs["vp"]), jnp.stack(outs["cp"]),
            jnp.stack(outs["ks"]), jnp.stack(outs["vs"]), jnp.stack(outs["cs"]))
```

```python
import functools

import jax
import jax.numpy as jnp
from jax import lax
from jax.experimental import pallas as pl
from jax.experimental.pallas import tpu as pltpu

_F32 = jnp.float32
_BF16 = jnp.bfloat16

LN_EPS = 1e-5
TOP_K = 2
_VMEM_LIMIT_V7X = 56 * 1024 * 1024
_LANES = 128
_SUBLANES = 8
_MOE_ROWS = 256


def _params(*sem):
    return pltpu.CompilerParams(dimension_semantics=sem, vmem_limit_bytes=_VMEM_LIMIT_V7X)


def _blk(dim, pref):
    if dim <= pref:
        return dim
    assert dim % pref == 0, (dim, pref)
    return pref


def _split_bf16(x):
    hi = x.astype(_BF16)
    lo = (x - hi.astype(_F32)).astype(_BF16)
    return hi, lo


def _dot(x, w):
    assert x.dtype == w.dtype, (x.dtype, w.dtype)
    d = functools.partial(jnp.dot, preferred_element_type=_F32)
    if x.dtype == _BF16:
        return d(x, w)
    xh, xl = _split_bf16(x)
    wh, wl = _split_bf16(w)
    return d(xh, wh) + (d(xl, wh) + d(xh, wl))


def _sigmoid(x):
    return 1.0 / (1.0 + jnp.exp(-x))


def _layer_norm(v, g, b):
    mu = jnp.mean(v, axis=-1, keepdims=True)
    d = v - mu
    var = jnp.mean(d * d, axis=-1, keepdims=True)
    return d * lax.rsqrt(var + LN_EPS) * g + b


def _mm_kernel(x_ref, w_ref, o_ref, *, scale):
    acc = _dot(x_ref[...], w_ref[...])
    if scale != 1.0:
        acc = acc * scale
    o_ref[...] = acc.astype(o_ref.dtype)


def _mm2_kernel(x_ref, w_ref, o_ref, ob_ref):
    acc = _dot(x_ref[...], w_ref[...])
    o_ref[...] = acc
    ob_ref[...] = acc.astype(ob_ref.dtype)


def _glu_kernel(x_ref, wv_ref, wg_ref, o_ref):
    x = x_ref[...]
    o_ref[...] = _dot(x, wv_ref[...]) * _sigmoid(_dot(x, wg_ref[...]))


def _gate_kernel(x_ref, w_ref, b_ref, o_ref):
    o_ref[...] = _sigmoid(_dot(x_ref[...], w_ref[...]) + b_ref[...])


def _mix_kernel(a_ref, c_ref, wa_ref, wc_ref, ga_ref, gc_ref, o_ref):
    att = _dot(a_ref[...], wa_ref[...])
    cnv = _dot(c_ref[...], wc_ref[...])
    o_ref[...] = (ga_ref[...] * att + gc_ref[...] * cnv).astype(o_ref.dtype)


def _row_spec(bm, k):
    return pl.BlockSpec((bm, k), lambda j, i: (i, 0))


def _col_spec(k, bn, off_blocks=0):
    return pl.BlockSpec((k, bn), lambda j, i: (0, j + off_blocks))


def _tile_spec(bm, bn, off_blocks=0):
    return pl.BlockSpec((bm, bn), lambda j, i: (i, j + off_blocks))


def _dense_call(kernel, name, m, n, bm, bn, in_specs, out_dtypes, args):
    outs = [jax.ShapeDtypeStruct((m, n), dt) for dt in out_dtypes]
    specs = [_tile_spec(bm, bn) for _ in out_dtypes]
    single = len(outs) == 1
    return pl.pallas_call(
        kernel, grid=(n // bn, m // bm), in_specs=in_specs,
        out_specs=specs[0] if single else specs,
        out_shape=outs[0] if single else outs,
        compiler_params=_params("parallel", "parallel"), name=name)(*args)


def _dense_blocks(x, n):
    m, k = x.shape
    prec3 = x.dtype == _F32
    bm = _blk(m, 512)
    bn = _blk(n, 256 if prec3 else 1024)
    return m, k, bm, bn


def _project(x, w, col0, n, out_dtype, scale=1.0, name="proj"):
    m, k, bm, bn = _dense_blocks(x, n)
    return _dense_call(functools.partial(_mm_kernel, scale=scale), name, m, n, bm, bn,
                       [_row_spec(bm, k), _col_spec(k, bn, col0 // bn)], [out_dtype], (x, w))


def _project_kv(x, w, col0, n, name):
    m, k, bm, bn = _dense_blocks(x, n)
    return _dense_call(_mm2_kernel, name, m, n, bm, bn,
                       [_row_spec(bm, k), _col_spec(k, bn, col0 // bn)], [_F32, _BF16], (x, w))


def _project_glu(x, w, col0, c, name="glu"):
    m, k, bm, bn = _dense_blocks(x, c)
    return _dense_call(_glu_kernel, name, m, c, bm, bn,
                       [_row_spec(bm, k), _col_spec(k, bn, col0 // bn), _col_spec(k, bn, (col0 + c) // bn)],
                       [_F32], (x, w, w))


def _project_gate(x, w, col0, n, b_gate, name="gate"):
    m, k, bm, bn = _dense_blocks(x, n)
    b_spec = pl.BlockSpec((1, bn), lambda j, i: (0, j))
    return _dense_call(_gate_kernel, name, m, n, bm, bn,
                       [_row_spec(bm, k), _col_spec(k, bn, col0 // bn), b_spec], [_F32],
                       (x, w, b_gate.reshape(1, n)))


def _mix(att, cnv, w_att, w_cnv, gates, out_dtype, name="mix"):
    n = w_att.shape[1]
    m, ka, bm, bn = _dense_blocks(att, n)
    kc = cnv.shape[1]
    return _dense_call(_mix_kernel, name, m, n, bm, bn,
                       [_row_spec(bm, ka), _row_spec(bm, kc), _col_spec(ka, bn), _col_spec(kc, bn),
                        _tile_spec(bm, bn), _tile_spec(bm, bn, n // bn)],
                       [out_dtype], (att, cnv, w_att, w_cnv, gates, gates))


_LN_ROWS = 256


def _res_ln_kernel(x_ref, r_ref, g_ref, b_ref, *rest, alpha, n_blocks):
    o_ref = rest[-1]
    i = pl.program_id(0)

    @pl.when(i < n_blocks)
    def _():
        o_ref[...] = _layer_norm(alpha * x_ref[...] + r_ref[...], g_ref[...], b_ref[...])

    @pl.when(i >= n_blocks)
    def _():
        o_ref[...] = jnp.zeros_like(o_ref)


def _res_ln(x, r, g, b, alpha, total_rows, row0=0, into=None, name="res_ln"):
    m, d = x.shape
    bt = _blk(m, _LN_ROWS)
    assert row0 % bt == 0 and total_rows % bt == 0
    n_blocks = m // bt
    row = pl.BlockSpec((bt, d), lambda i: (jnp.minimum(i, n_blocks - 1), 0))
    vec = pl.BlockSpec((1, d), lambda i: (0, 0))
    args = [x, r, g.reshape(1, d), b.reshape(1, d)]
    in_specs = [row, row, vec, vec]
    aliases = {}
    grid = total_rows // bt
    if into is not None:
        assert into.shape == (total_rows, d)
        args.append(into)
        in_specs.append(pl.BlockSpec(memory_space=pl.ANY))
        aliases = {4: 0}
        grid = n_blocks
    else:
        assert row0 == 0
    return pl.pallas_call(
        functools.partial(_res_ln_kernel, alpha=alpha, n_blocks=n_blocks), grid=(grid,),
        in_specs=in_specs, out_specs=pl.BlockSpec((bt, d), lambda i: (i + row0 // bt, 0)),
        out_shape=jax.ShapeDtypeStruct((total_rows, d), _F32), input_output_aliases=aliases,
        compiler_params=_params("parallel"), name=name)(*args)


def _router_kernel(x_ref, w_ref, b_ref, o_ref, *, n_groups, per_group):
    logits = _dot(x_ref[...], w_ref[...]) + b_ref[...]
    lane = lax.broadcasted_iota(jnp.int32, logits.shape, 1).astype(_F32)
    neg = -jnp.inf
    big = 1e9

    def arg_max(v):
        m = jnp.max(v, axis=-1, keepdims=True)
        return m, jnp.min(jnp.where(v == m, lane, big), axis=-1, keepdims=True)

    glog = jnp.where(lane < n_groups, logits, neg)
    gmax, g_idx = arg_max(glog)
    g_p = 1.0 / jnp.sum(jnp.exp(glog - gmax), axis=-1, keepdims=True)
    lo = n_groups + g_idx * per_group
    elog = jnp.where((lane >= lo) & (lane < lo + per_group), logits, neg)
    m1, i1 = arg_max(elog)
    m2, i2 = arg_max(jnp.where(lane == i1, neg, elog))
    e2 = jnp.exp(m2 - m1)
    p1 = 1.0 / (1.0 + e2)
    p2 = e2 / (1.0 + e2)
    o_ref[...] = jnp.where(lane == 0, i1 - n_groups,
                 jnp.where(lane == 1, i2 - n_groups,
                 jnp.where(lane == 2, g_p * p1,
                 jnp.where(lane == 3, g_p * p2, 0.0))))


def _router(x1, row0, m, w_r, b_r, n_groups, per_group, name="router"):
    d = x1.shape[1]
    bt = _blk(m, 256)
    assert row0 % bt == 0
    return pl.pallas_call(
        functools.partial(_router_kernel, n_groups=n_groups, per_group=per_group), grid=(m // bt,),
        in_specs=[pl.BlockSpec((bt, d), lambda i: (i + row0 // bt, 0)),
                  pl.BlockSpec((d, _LANES), lambda i: (0, 0)),
                  pl.BlockSpec((1, _LANES), lambda i: (0, 0))],
        out_specs=pl.BlockSpec((bt, _LANES), lambda i: (i, 0)),
        out_shape=jax.ShapeDtypeStruct((m, _LANES), _F32),
        compiler_params=_params("parallel"), name=name)(x1, w_r, b_r)


_LOG2E = 1.4426950408889634
_SAMPLE_PAGES_PER_STEP = 4


def _neg_abs(z):
    return lax.bitcast_convert_type(lax.bitcast_convert_type(z, jnp.uint32) | jnp.uint32(0x80000000), _F32)


def _neg_softplus2(z):
    return jnp.maximum(z, 0.0) + jnp.log2(1.0 + jnp.exp2(_neg_abs(z)))


_MASKED_LOGIT = -1e30


def _neg_suffix_matrix(n, classes=1, copies=2):
    r = lax.broadcasted_iota(jnp.int32, (copies * n, n), 0) & (n - 1)
    c = lax.broadcasted_iota(jnp.int32, (copies * n, n), 1)
    keep = r >= c
    if classes > 1:
        keep = keep & ((r & (classes - 1)) == (c & (classes - 1)))
    return jnp.where(keep, -1.0, 0.0).astype(_BF16)


def _attn_prompt_kernel(bias_ref, q_ref, k_ref, v_ref, o_ref, acc_ref, carry_ref, z0_ref, z1_ref, s0_ref, s1_ref,
                        *, bq, sub):
    h = pl.program_id(0)
    qi = pl.program_id(1)
    bias = bias_ref[h]
    nsub = bq // sub
    neg_suffix = _neg_suffix_matrix(sub, copies=1)
    acc_ref[...] = jnp.zeros_like(acc_ref)
    carry_ref[...] = jnp.zeros_like(carry_ref)

    def front(j, z_ref, s_ref, masked=False):
        k0 = pl.multiple_of(j * bq, bq)
        z = lax.dot_general(q_ref[...], k_ref[pl.ds(k0, bq), :], (((1,), (1,)), ((), ())),
                            preferred_element_type=_F32) + bias
        if masked:
            q_pos = qi * bq + lax.broadcasted_iota(jnp.int32, (bq, bq), 0)
            k_pos = k0 + lax.broadcasted_iota(jnp.int32, (bq, bq), 1)
            z = jnp.where(k_pos < q_pos, z, _MASKED_LOGIT)
        z_ref[...] = z
        for s in range(nsub):
            s_ref[s * bq:(s + 1) * bq, :] = _neg_softplus2(z[:, s * sub:(s + 1) * sub]).astype(_BF16)

    def back(j, z_ref, s_ref):
        vb = v_ref[pl.ds(pl.multiple_of(j * bq, bq), bq), :]
        incl = jnp.dot(s_ref[...], neg_suffix, preferred_element_type=_F32)
        carry = carry_ref[...]
        ws = [None] * nsub
        for s in range(nsub - 1, -1, -1):
            inc = incl[s * bq:(s + 1) * bq, :]
            ws[s] = jnp.exp2(z_ref[:, s * sub:(s + 1) * sub] + inc + carry).astype(_BF16)
            carry = carry + inc[:, 0:1]
        carry_ref[...] = carry
        acc_ref[...] += jnp.dot(jnp.concatenate(ws, axis=1), vb, preferred_element_type=_F32)

    front(qi, z0_ref, s0_ref, masked=True)

    def pair(t, _):
        c = qi - 2 * t
        front(c - 1, z1_ref, s1_ref)
        back(c, z0_ref, s0_ref)
        front(c - 2, z0_ref, s0_ref)
        back(c - 1, z1_ref, s1_ref)
        return 0

    lax.fori_loop(0, qi // 2, pair, 0)

    @pl.when(qi % 2 == 1)
    def _():
        front(0, z1_ref, s1_ref)
        back(1, z0_ref, s0_ref)
        back(0, z1_ref, s1_ref)

    @pl.when(qi % 2 == 0)
    def _():
        back(0, z0_ref, s0_ref)

    o_ref[...] = acc_ref[...].astype(o_ref.dtype)


def _attn_prompt(q, k, v, bias, n_heads, name="attn_prompt"):
    t, a = q.shape
    dh = a // n_heads
    bq = _blk(t, 512)
    sub = _blk(bq, 256)
    assert sub & (sub - 1) == 0
    bias = bias * _LOG2E
    kernel = functools.partial(_attn_prompt_kernel, bq=bq, sub=sub)
    return pl.pallas_call(
        kernel, grid=(n_heads, t // bq),
        in_specs=[pl.BlockSpec(memory_space=pltpu.SMEM),
                  pl.BlockSpec((bq, dh), lambda h, i: (i, h)),
                  pl.BlockSpec((t, dh), lambda h, i: (0, h)),
                  pl.BlockSpec((t, dh), lambda h, i: (0, h))],
        out_specs=pl.BlockSpec((bq, dh), lambda h, i: (i, h)),
        out_shape=jax.ShapeDtypeStruct((t, a), _BF16),
        scratch_shapes=[pltpu.VMEM((bq, dh), _F32), pltpu.VMEM((bq, 1), _F32),
                        pltpu.VMEM((bq, bq), _F32), pltpu.VMEM((bq, bq), _F32),
                        pltpu.VMEM((bq * bq // sub, sub), _BF16), pltpu.VMEM((bq * bq // sub, sub), _BF16)],
        compiler_params=_params("parallel", "parallel"), name=name)(bias, q, k, v)


def _attn_sample_kernel(pt_ref, q_ref, bias_ref, *refs, n_heads, n_steps, pages):
    k_refs, v_refs = refs[:pages], refs[pages:2 * pages]
    o_ref, acc_ref, carry_ref = refs[2 * pages:]
    p = pl.program_id(1)
    hh = n_heads
    cols = k_refs[0].shape[0]
    nblk = cols // _LANES

    @pl.when(p == 0)
    def _():
        acc_ref[...] = jnp.zeros_like(acc_ref)
        carry_ref[...] = jnp.zeros_like(carry_ref)

    q = q_ref[...].astype(_BF16)
    bias = bias_ref[...]
    tot_r = lax.broadcasted_iota(jnp.int32, (2 * _LANES, _LANES), 0)
    tot_c = lax.broadcasted_iota(jnp.int32, (2 * _LANES, _LANES), 1)
    neg_total = jnp.where((tot_r & (hh - 1)) == (tot_c & (hh - 1)), -1.0, 0.0).astype(_BF16)
    sums_mat = jnp.concatenate([_neg_suffix_matrix(_LANES, classes=hh), neg_total], axis=1)
    row = lax.broadcasted_iota(jnp.int32, (hh, cols), 0)
    col = lax.broadcasted_iota(jnp.int32, (hh, cols), 1)
    real = (col & (hh - 1)) == row

    suffix = carry_ref[...]
    acc = acc_ref[...]
    for s in range(pages):
        kp = k_refs[s][...].astype(_BF16)
        vp = v_refs[s][...].astype(_BF16)
        z = lax.dot_general(q, kp, (((1,), (1,)), ((), ())), preferred_element_type=_F32) + bias
        sp = _neg_softplus2(z)
        stacked = jnp.concatenate([sp[:, i * _LANES:(i + 1) * _LANES] for i in range(nblk)], axis=0)
        hi, lo = _split_bf16(stacked)
        sums = jnp.dot(jnp.concatenate([hi, lo], axis=1), sums_mat, preferred_element_type=_F32)
        later = [None] * nblk
        for i in range(nblk - 1, -1, -1):
            later[i] = sums[i * hh:(i + 1) * hh, :_LANES] + suffix
            suffix = suffix + sums[i * hh:(i + 1) * hh, _LANES:]
        w = jnp.exp2(z + jnp.concatenate(later, axis=1))
        acc = acc + jnp.dot(jnp.where(real, w, 0.0).astype(_BF16), vp, preferred_element_type=_F32)
    carry_ref[...] = suffix
    acc_ref[...] = acc

    @pl.when(p == n_steps - 1)
    def _():
        o_ref[...] = acc


def _attn_sample(q, cache_k, cache_v, page_table, bias, layer, name="attn_sample"):
    b, hh, dh = q.shape
    assert hh & (hh - 1) == 0 and _LANES % hh == 0
    depth, n_phys, page = cache_k.shape[:3]
    n_pages = page_table.shape[1]
    ck = cache_k.reshape(depth, n_phys, page * hh, dh)
    cv = cache_v.reshape(depth, n_phys, page * hh, dh)
    pages = _SAMPLE_PAGES_PER_STEP if n_pages % _SAMPLE_PAGES_PER_STEP == 0 else 1
    n_steps = n_pages // pages

    def page_spec(s):
        return pl.BlockSpec((None, None, page * hh, dh),
                            lambda i, p, pt: (layer, pt[i * n_pages + (n_pages - 1 - (p * pages + s))], 0, 0))

    page_specs = [page_spec(s) for s in range(pages)]
    grid_spec = pltpu.PrefetchScalarGridSpec(
        num_scalar_prefetch=1, grid=(b, n_steps),
        in_specs=[pl.BlockSpec((None, hh, dh), lambda i, p, pt: (i, 0, 0)),
                  pl.BlockSpec((hh, 1), lambda i, p, pt: (0, 0))] + page_specs + page_specs,
        out_specs=pl.BlockSpec((None, hh, dh), lambda i, p, pt: (i, 0, 0)),
        scratch_shapes=[pltpu.VMEM((hh, dh), _F32), pltpu.VMEM((hh, _LANES), _F32)])
    return pl.pallas_call(
        functools.partial(_attn_sample_kernel, n_heads=hh, n_steps=n_steps, pages=pages),
        grid_spec=grid_spec, out_shape=jax.ShapeDtypeStruct((b, hh, dh), _F32),
        compiler_params=_params("parallel", "arbitrary"), name=name)(
            page_table.reshape(-1), q, (bias * _LOG2E).reshape(hh, 1), *([ck] * pages), *([cv] * pages))


_HALO = 32


def _conv_prompt_kernel(u_ref, prev_ref, hist_ref, w_ref, b_ref, g_ref, beta_ref, o_ref, win_ref, y_ref,
                        *, taps, tb, cw):
    i = pl.program_id(0)
    bt, ch = u_ref.shape
    lead = _HALO - (taps - 1)
    win_ref[0:_HALO, :] = jnp.where(i == 0, hist_ref[...], prev_ref[...])
    win_ref[_HALO:, :] = u_ref[...]
    by_residue = [[k for k in range(taps) if (k + lead) % _SUBLANES == res] for res in range(_SUBLANES)]
    for c0 in range(0, ch, cw):
        for t0 in range(0, bt, tb):
            acc = jnp.zeros((tb // _SUBLANES, _SUBLANES, cw), _F32)
            for res, ks in enumerate(by_residue):
                if not ks:
                    continue
                reach = ks[-1] + lead - res
                shifted = win_ref[t0 + res:t0 + res + tb + reach, c0:c0 + cw]
                shifted = shifted.reshape((tb + reach) // _SUBLANES, _SUBLANES, cw)
                for k in ks:
                    a0 = (k + lead - res) // _SUBLANES
                    acc = acc + shifted[a0:a0 + tb // _SUBLANES] * w_ref[k, :, c0:c0 + cw][None]
            y_ref[t0:t0 + tb, c0:c0 + cw] = acc.reshape(tb, cw)
    y = _layer_norm(y_ref[...] + b_ref[...], g_ref[...], beta_ref[...])
    o_ref[...] = (y * _sigmoid(y)).astype(o_ref.dtype)


def _conv_prompt(u, hist, conv_w, conv_b, ln_g, ln_b, out_dtype, name="conv_prompt"):
    t, ch = u.shape
    taps = conv_w.shape[0]
    assert taps - 1 <= _HALO and t % _HALO == 0
    bt = _blk(t, 128)
    hist_pad = jnp.concatenate([jnp.zeros((_HALO - (taps - 1), ch), u.dtype), hist.astype(u.dtype)], axis=0)
    ratio = bt // _HALO
    vec = pl.BlockSpec((1, ch), lambda i: (0, 0))
    kernel = functools.partial(_conv_prompt_kernel, taps=taps, tb=min(bt, 64), cw=min(ch, 256))
    return pl.pallas_call(
        kernel, grid=(t // bt,),
        in_specs=[pl.BlockSpec((bt, ch), lambda i: (i, 0)),
                  pl.BlockSpec((_HALO, ch), lambda i: (jnp.maximum(i * ratio - 1, 0), 0)),
                  pl.BlockSpec((_HALO, ch), lambda i: (0, 0)),
                  pl.BlockSpec((taps, _SUBLANES, ch), lambda i: (0, 0, 0)), vec, vec, vec],
        out_specs=pl.BlockSpec((bt, ch), lambda i: (i, 0)),
        out_shape=jax.ShapeDtypeStruct((t, ch), out_dtype),
        scratch_shapes=[pltpu.VMEM((bt + _HALO, ch), _F32), pltpu.VMEM((bt, ch), _F32)],
        compiler_params=_params("parallel"), name=name)(
            u, u, hist_pad, jnp.broadcast_to(conv_w[:, None, :], (taps, _SUBLANES, ch)),
            conv_b.reshape(1, ch), ln_g.reshape(1, ch), ln_b.reshape(1, ch))


def _conv_sample_kernel(u_ref, s_ref, w_ref, b_ref, g_ref, beta_ref, o_ref, ns_ref, *, taps):
    s = s_ref[...]
    u = u_ref[...]
    conv = jnp.sum(s * w_ref[0:taps - 1, :], axis=0, keepdims=True) + u * w_ref[taps - 1:taps, :]
    y = _layer_norm(conv + b_ref[...], g_ref[...], beta_ref[...])
    o_ref[...] = y * _sigmoid(y)
    ns_ref[0:taps - 2, :] = s_ref[1:taps - 1, :]
    ns_ref[taps - 2:taps - 1, :] = u


def _conv_sample(u, state, conv_w, conv_b, ln_g, ln_b, name="conv_sample"):
    b, ch = u.shape
    taps = conv_w.shape[0]
    vec = pl.BlockSpec((1, ch), lambda i: (0, 0))
    row = pl.BlockSpec((None, 1, ch), lambda i: (i, 0, 0))
    st = pl.BlockSpec((None, taps - 1, ch), lambda i: (i, 0, 0))
    out, new_state = pl.pallas_call(
        functools.partial(_conv_sample_kernel, taps=taps), grid=(b,),
        in_specs=[row, st, pl.BlockSpec((taps, ch), lambda i: (0, 0)), vec, vec, vec],
        out_specs=[row, st],
        out_shape=[jax.ShapeDtypeStruct((b, 1, ch), _F32), jax.ShapeDtypeStruct((b, taps - 1, ch), _F32)],
        compiler_params=_params("parallel"), name=name)(
            u.reshape(b, 1, ch), state, conv_w, conv_b.reshape(1, ch), ln_g.reshape(1, ch), ln_b.reshape(1, ch))
    return out.reshape(b, ch), new_state


_DMA_UNROLL = 8


def _row_copy(src_hbm, src_row, dst_ref, dst_row, sem):
    return pltpu.make_async_copy(src_hbm.at[pl.ds(src_row, 1), :], dst_ref.at[pl.ds(dst_row, 1), :], sem)


def _gather_kernel(tok_ref, na_ref, x_hbm, o_ref, buf_ref, sem, *, rows):
    i = pl.program_id(0)
    n_active = na_ref[0]
    slot = i % 2

    def start(blk, s):
        def body(r, _):
            _row_copy(x_hbm, tok_ref[blk * rows + r], buf_ref.at[s], r, sem.at[s]).start()
            return 0
        lax.fori_loop(0, rows, body, 0, unroll=_DMA_UNROLL)

    @pl.when((i == 0) & (n_active > 0))
    def _():
        start(0, 0)

    @pl.when(i + 1 < n_active)
    def _():
        start(i + 1, 1 - slot)

    @pl.when(i < n_active)
    def _():
        def body(r, _):
            _row_copy(x_hbm, 0, buf_ref.at[slot], r, sem.at[slot]).wait()
            return 0
        lax.fori_loop(0, rows, body, 0, unroll=_DMA_UNROLL)
        o_ref[...] = buf_ref[slot].astype(o_ref.dtype)

    @pl.when(i >= n_active)
    def _():
        o_ref[...] = jnp.zeros_like(o_ref)


def _gather_rows(x, row_tok, n_active, rows, name="moe_gather"):
    n_rows = row_tok.shape[0]
    d = x.shape[1]
    grid_spec = pltpu.PrefetchScalarGridSpec(
        num_scalar_prefetch=2, grid=(n_rows // rows,),
        in_specs=[pl.BlockSpec(memory_space=pl.ANY)],
        out_specs=pl.BlockSpec((rows, d), lambda i, tok, na: (i, 0)),
        scratch_shapes=[pltpu.VMEM((2, rows, d), x.dtype), pltpu.SemaphoreType.DMA((2,))])
    return pl.pallas_call(
        functools.partial(_gather_kernel, rows=rows), grid_spec=grid_spec,
        out_shape=jax.ShapeDtypeStruct((n_rows, d), _BF16),
        compiler_params=_params("arbitrary"), name=name)(row_tok, n_active, x)


def _expert_changed(be_ref, i):
    prev = be_ref[jnp.maximum(i - 1, 0)]
    return (i == 0) | (be_ref[i] != prev)


def _gmm1_kernel(be_ref, na_ref, x_ref, w1_ref, w3_ref, o_ref, w1b_ref, w3b_ref):
    i = pl.program_id(1)

    @pl.when(i < na_ref[0])
    def _():
        @pl.when(_expert_changed(be_ref, i))
        def _():
            w1b_ref[...] = w1_ref[...].astype(_BF16)
            w3b_ref[...] = w3_ref[...].astype(_BF16)

        x = x_ref[...]
        a = _dot(x, w1b_ref[...])
        o_ref[...] = (a * _sigmoid(a) * _dot(x, w3b_ref[...])).astype(o_ref.dtype)

    @pl.when(i >= na_ref[0])
    def _():
        o_ref[...] = jnp.zeros_like(o_ref)


def _gmm2_kernel(be_ref, na_ref, h_ref, w2_ref, o_ref, w2b_ref):
    i = pl.program_id(1)

    @pl.when(i < na_ref[0])
    def _():
        @pl.when(_expert_changed(be_ref, i))
        def _():
            w2b_ref[...] = w2_ref[...].astype(_BF16)

        o_ref[...] = _dot(h_ref[...], w2b_ref[...])

    @pl.when(i >= na_ref[0])
    def _():
        o_ref[...] = jnp.zeros_like(o_ref)


def _expert_ffn(xs, block_e, n_active, w1, w3, w2, layer):
    n_rows, d = xs.shape
    f = w1.shape[-1]
    bm = _MOE_ROWS
    tn1 = _blk(f, 256)
    tn2 = _blk(d, 1024)
    w_in_spec = pl.BlockSpec((None, None, d, tn1), lambda j, i, be, na: (layer, be[i], 0, j))
    hidden = pl.pallas_call(
        _gmm1_kernel,
        grid_spec=pltpu.PrefetchScalarGridSpec(
            num_scalar_prefetch=2, grid=(f // tn1, n_rows // bm),
            in_specs=[pl.BlockSpec((bm, d), lambda j, i, be, na: (i, 0)), w_in_spec, w_in_spec],
            out_specs=pl.BlockSpec((bm, tn1), lambda j, i, be, na: (i, j)),
            scratch_shapes=[pltpu.VMEM((d, tn1), _BF16), pltpu.VMEM((d, tn1), _BF16)]),
        out_shape=jax.ShapeDtypeStruct((n_rows, f), _BF16),
        compiler_params=_params("arbitrary", "arbitrary"), name="moe_up")(block_e, n_active, xs, w1, w3)
    return pl.pallas_call(
        _gmm2_kernel,
        grid_spec=pltpu.PrefetchScalarGridSpec(
            num_scalar_prefetch=2, grid=(d // tn2, n_rows // bm),
            in_specs=[pl.BlockSpec((bm, f), lambda j, i, be, na: (i, 0)),
                      pl.BlockSpec((None, None, f, tn2), lambda j, i, be, na: (layer, be[i], 0, j))],
            out_specs=pl.BlockSpec((bm, tn2), lambda j, i, be, na: (i, j)),
            scratch_shapes=[pltpu.VMEM((f, tn2), _BF16)]),
        out_shape=jax.ShapeDtypeStruct((n_rows, d), _F32),
        compiler_params=_params("arbitrary", "arbitrary"), name="moe_down")(block_e, n_active, hidden, w2)


def _combine_kernel(pos_ref, x_ref, r_ref, g_ref, b_ref, ys_hbm, o_ref, buf_ref, sem, *, bt, alpha):
    i = pl.program_id(0)
    slot = i % 2

    def start(blk, s):
        def body(r, _):
            for k in range(TOP_K):
                _row_copy(ys_hbm, pos_ref[(blk * bt + r) * TOP_K + k], buf_ref.at[s], k * bt + r, sem.at[s]).start()
            return 0
        lax.fori_loop(0, bt, body, 0, unroll=_DMA_UNROLL)

    @pl.when(i == 0)
    def _():
        start(0, 0)

    @pl.when(i + 1 < pl.num_programs(0))
    def _():
        start(i + 1, 1 - slot)

    def drain(r, _):
        _row_copy(ys_hbm, 0, buf_ref.at[slot], r, sem.at[slot]).wait()
        return 0

    lax.fori_loop(0, TOP_K * bt, drain, 0, unroll=_DMA_UNROLL)
    slab = r_ref[...]
    ffn = buf_ref[slot, 0:bt, :] * slab[:, 2:3]
    for k in range(1, TOP_K):
        ffn = ffn + buf_ref[slot, k * bt:(k + 1) * bt, :] * slab[:, 2 + k:3 + k]
    o_ref[...] = _layer_norm(alpha * x_ref[...] + ffn, g_ref[...], b_ref[...])


def _combine_ln(x1, row0, m, slab, pos, ys, g, b, alpha, name="moe_combine"):
    d = x1.shape[1]
    bt = _blk(m, 128)
    assert row0 % bt == 0
    vec = pl.BlockSpec((1, d), lambda i, pos: (0, 0))
    grid_spec = pltpu.PrefetchScalarGridSpec(
        num_scalar_prefetch=1, grid=(m // bt,),
        in_specs=[pl.BlockSpec((bt, d), lambda i, pos: (i + row0 // bt, 0)),
                  pl.BlockSpec((bt, _LANES), lambda i, pos: (i, 0)), vec, vec,
                  pl.BlockSpec(memory_space=pl.ANY)],
        out_specs=pl.BlockSpec((bt, d), lambda i, pos: (i, 0)),
        scratch_shapes=[pltpu.VMEM((2, TOP_K * bt, d), _F32), pltpu.SemaphoreType.DMA((2,))])
    return pl.pallas_call(
        functools.partial(_combine_kernel, bt=bt, alpha=alpha), grid_spec=grid_spec,
        out_shape=jax.ShapeDtypeStruct((m, d), _F32),
        compiler_params=_params("arbitrary"), name=name)(pos, x1, slab, g.reshape(1, d), b.reshape(1, d), ys)


def _dispatch_plan(expert, n_experts, bm):
    a = expert.shape[0]
    n_blocks = -(-(a + n_experts * (bm - 1)) // bm)
    n_rows = n_blocks * bm
    onehot = (expert[:, None] == jnp.arange(n_experts, dtype=jnp.int32)[None, :]).astype(jnp.int32)
    rank = jnp.take_along_axis(jnp.cumsum(onehot, axis=0) - onehot, expert[:, None], axis=1)[:, 0]
    counts = jnp.sum(onehot, axis=0)
    padded = (counts + bm - 1) // bm * bm
    pad_end = jnp.cumsum(padded)
    pad_start = pad_end - padded
    dest = (pad_start[expert] + rank).astype(jnp.int32)
    row_src = jnp.full((n_rows,), -1, jnp.int32).at[dest].set(jnp.arange(a, dtype=jnp.int32))
    n_active = (pad_end[-1] // bm).astype(jnp.int32)
    blk = jnp.minimum(jnp.arange(n_blocks, dtype=jnp.int32), jnp.maximum(n_active - 1, 0))
    block_e = jnp.minimum(jnp.searchsorted(pad_end, blk * bm, side="right"), n_experts - 1).astype(jnp.int32)
    return dest, row_src, block_e, n_active.reshape(1)


def _branches(x, w_in, w_att_out, w_conv_out, w_out, b_gate, attn_fn, conv_fn, widths, kv_bf16):
    a, c, d = widths
    scale = float(a // attn_fn.n_heads) ** -0.5 * _LOG2E
    act = x.dtype
    q = _project(x, w_in, 0, a, act, scale=scale, name="proj_q")
    if kv_bf16:
        k, k_b = _project_kv(x, w_in, a, a, "proj_k")
        v, v_b = _project_kv(x, w_in, 2 * a, a, "proj_v")
    else:
        k = k_b = _project(x, w_in, a, a, _F32, name="proj_k")
        v = v_b = _project(x, w_in, 2 * a, a, _F32, name="proj_v")
    u = _project_glu(x, w_in, 3 * a, c)
    gates = _project_gate(x, w_in, 3 * a + 2 * c, 2 * d, b_gate)
    att = attn_fn(q, k_b, v_b)
    cnv, conv_state = conv_fn(u)
    pre = _mix(att, cnv, w_att_out, w_conv_out, gates, act)
    mixed = _project(pre, w_out, 0, d, _F32, name="proj_out")
    return mixed, k, v, conv_state


class _PromptAttn:
    def __init__(self, bias, n_heads):
        self.bias, self.n_heads = bias, n_heads

    def __call__(self, q, k, v):
        return _attn_prompt(q, k, v, self.bias, self.n_heads)


class _SampleAttn:
    def __init__(self, bias, n_heads, cache_k, cache_v, page_table, layer):
        self.bias, self.n_heads = bias, n_heads
        self.cache_k, self.cache_v, self.page_table, self.layer = cache_k, cache_v, page_table, layer

    def __call__(self, q, k, v):
        b, a = q.shape
        out = _attn_sample(q.reshape(b, self.n_heads, a // self.n_heads), self.cache_k, self.cache_v,
                           self.page_table, self.bias, self.layer)
        return out.reshape(b, a)


def kernel(x_prompt, x_sample, cache_k, cache_v, state_conv, page_table, w_in, b_sb, b_gate, conv_w, conv_b,
           conv_ln_g, conv_ln_b, w_att_out, w_conv_out, w_out, ln1_g, ln1_b, w_router_group, b_router_group,
           w_router_expert, b_router_expert, w1, w3, w2, ln2_g, ln2_b):
    depth, d, _ = w_in.shape
    n_heads, d_head = cache_k.shape[3], cache_k.shape[4]
    a = n_heads * d_head
    c = conv_w.shape[2]
    taps = conv_w.shape[1]
    n_groups = w_router_group.shape[2]
    n_experts = w_router_expert.shape[2]
    per_group = n_experts // n_groups
    assert n_groups + n_experts <= _LANES
    alpha = (2.0 * depth) ** 0.25
    bsz, seq, _ = x_prompt.shape
    dec, dec_seq, _ = x_sample.shape
    assert bsz == 1 and dec_seq == 1
    n_p, n_s = bsz * seq, dec * dec_seq

    y_p = x_prompt.reshape(n_p, d)
    y_s = x_sample.reshape(n_s, d)
    outs = {k: [] for k in ("kp", "vp", "cp", "ks", "vs", "cs")}
    for l in range(depth):
        win, wao, wco, wo = w_in[l], w_att_out[l], w_conv_out[l], w_out[l]
        hist0 = jnp.zeros((taps - 1, c), _F32)
        conv_p = lambda u, l=l, hist0=hist0: (
            _conv_prompt(u, hist0, conv_w[l], conv_b[l], conv_ln_g[l], conv_ln_b[l], _BF16), u[seq - (taps - 1):])
        mixed_p, k_p, v_p, cs_p = _branches(
            y_p.astype(_BF16), win.astype(_BF16), wao.astype(_BF16), wco.astype(_BF16), wo.astype(_BF16),
            b_gate[l], _PromptAttn(b_sb[l], n_heads), conv_p, (a, c, d), True)
        conv_s = lambda u, l=l: _conv_sample(u, state_conv[l], conv_w[l], conv_b[l], conv_ln_g[l], conv_ln_b[l])
        mixed_s, k_s, v_s, cs_s = _branches(
            y_s, win, wao, wco, wo, b_gate[l],
            _SampleAttn(b_sb[l], n_heads, cache_k, cache_v, page_table, l), conv_s, (a, c, d), False)

        n_all = n_p + -(-n_s // _LN_ROWS) * _LN_ROWS
        x1 = _res_ln(y_p, mixed_p, ln1_g[l], ln1_b[l], alpha, n_all, name="ln1_prompt")
        x1 = _res_ln(y_s, mixed_s, ln1_g[l], ln1_b[l], alpha, n_all, row0=n_p, into=x1, name="ln1_sample")

        w_r = jnp.zeros((d, _LANES), _F32).at[:, :n_groups].set(w_router_group[l]) \
                 .at[:, n_groups:n_groups + n_experts].set(w_router_expert[l])
        b_r = jnp.zeros((1, _LANES), _F32).at[0, :n_groups].set(b_router_group[l]) \
                 .at[0, n_groups:n_groups + n_experts].set(b_router_expert[l])
        slab_p = _router(x1, 0, n_p, w_r, b_r, n_groups, per_group, name="router_prompt")
        slab_s = _router(x1, n_p, n_s, w_r, b_r, n_groups, per_group, name="router_sample")
        expert = jnp.concatenate([slab_p[:, :TOP_K], slab_s[:, :TOP_K]], axis=0).astype(jnp.int32).reshape(-1)
        dest, row_src, block_e, n_active = _dispatch_plan(expert, n_experts, _MOE_ROWS)
        row_tok = jnp.maximum(row_src, 0) // TOP_K
        xs = _gather_rows(x1, row_tok, n_active, _MOE_ROWS)
        ys = _expert_ffn(xs, block_e, n_active, w1, w3, w2, l)
        y_p = _combine_ln(x1, 0, n_p, slab_p, dest[:n_p * TOP_K], ys, ln2_g[l], ln2_b[l], alpha,
                          name="combine_prompt")
        y_s = _combine_ln(x1, n_p, n_s, slab_s, dest[n_p * TOP_K:], ys, ln2_g[l], ln2_b[l], alpha,
                          name="combine_sample")

        outs["kp"].append(k_p.reshape(bsz, seq, n_heads, d_head))
        outs["vp"].append(v_p.reshape(bsz, seq, n_heads, d_head))
        outs["cp"].append(cs_p.reshape(bsz, taps - 1, c))
        outs["ks"].append(k_s.reshape(dec, dec_seq, n_heads, d_head))
        outs["vs"].append(v_s.reshape(dec, dec_seq, n_heads, d_head))
        outs["cs"].append(cs_s)
    return (y_p.reshape(bsz, seq, d), y_s.reshape(dec, dec_seq, d),
            jnp.stack(outs["kp"]), jnp.stack(outs["vp"]), jnp.stack(outs["cp"]),
            jnp.stack(outs["ks"]), jnp.stack(outs["vs"]), jnp.stack(outs["cs"]))
```

```python
import functools

import jax
import jax.numpy as jnp
from jax import lax
from jax.experimental import pallas as pl
from jax.experimental.pallas import tpu as pltpu

_F32 = jnp.float32
_BF16 = jnp.bfloat16

LN_EPS = 1e-5
TOP_K = 2
_VMEM_LIMIT_V7X = 56 * 1024 * 1024
_LANES = 128
_SUBLANES = 8
_MOE_ROWS = 256


def _params(*sem):
    return pltpu.CompilerParams(dimension_semantics=sem, vmem_limit_bytes=_VMEM_LIMIT_V7X)


def _blk(dim, pref):
    if dim <= pref:
        return dim
    assert dim % pref == 0, (dim, pref)
    return pref


def _split_bf16(x):
    hi = x.astype(_BF16)
    lo = (x - hi.astype(_F32)).astype(_BF16)
    return hi, lo


def _dot(x, w):
    assert x.dtype == w.dtype, (x.dtype, w.dtype)
    d = functools.partial(jnp.dot, preferred_element_type=_F32)
    if x.dtype == _BF16:
        return d(x, w)
    xh, xl = _split_bf16(x)
    wh, wl = _split_bf16(w)
    return d(xh, wh) + (d(xl, wh) + d(xh, wl))


def _sigmoid(x):
    return 1.0 / (1.0 + jnp.exp(-x))


def _layer_norm(v, g, b):
    mu = jnp.mean(v, axis=-1, keepdims=True)
    d = v - mu
    var = jnp.mean(d * d, axis=-1, keepdims=True)
    return d * lax.rsqrt(var + LN_EPS) * g + b


def _mm_kernel(x_ref, w_ref, o_ref, *, scale):
    acc = _dot(x_ref[...], w_ref[...])
    if scale != 1.0:
        acc = acc * scale
    o_ref[...] = acc.astype(o_ref.dtype)


def _mm2_kernel(x_ref, w_ref, o_ref, ob_ref):
    acc = _dot(x_ref[...], w_ref[...])
    o_ref[...] = acc
    ob_ref[...] = acc.astype(ob_ref.dtype)


def _glu_kernel(x_ref, wv_ref, wg_ref, o_ref):
    x = x_ref[...]
    o_ref[...] = _dot(x, wv_ref[...]) * _sigmoid(_dot(x, wg_ref[...]))


def _gate_kernel(x_ref, w_ref, b_ref, o_ref):
    o_ref[...] = _sigmoid(_dot(x_ref[...], w_ref[...]) + b_ref[...])


def _mix_kernel(a_ref, c_ref, wa_ref, wc_ref, ga_ref, gc_ref, o_ref):
    att = _dot(a_ref[...], wa_ref[...])
    cnv = _dot(c_ref[...], wc_ref[...])
    o_ref[...] = (ga_ref[...] * att + gc_ref[...] * cnv).astype(o_ref.dtype)


def _row_spec(bm, k):
    return pl.BlockSpec((bm, k), lambda j, i: (i, 0))


def _col_spec(k, bn, off_blocks=0):
    return pl.BlockSpec((k, bn), lambda j, i: (0, j + off_blocks))


def _tile_spec(bm, bn, off_blocks=0):
    return pl.BlockSpec((bm, bn), lambda j, i: (i, j + off_blocks))


def _dense_call(kernel, name, m, n, bm, bn, in_specs, out_dtypes, args):
    outs = [jax.ShapeDtypeStruct((m, n), dt) for dt in out_dtypes]
    specs = [_tile_spec(bm, bn) for _ in out_dtypes]
    single = len(outs) == 1
    return pl.pallas_call(
        kernel, grid=(n // bn, m // bm), in_specs=in_specs,
        out_specs=specs[0] if single else specs,
        out_shape=outs[0] if single else outs,
        compiler_params=_params("parallel", "parallel"), name=name)(*args)


def _dense_blocks(x, n):
    m, k = x.shape
    prec3 = x.dtype == _F32
    bm = _blk(m, 512)
    bn = _blk(n, 256 if prec3 else 1024)
    return m, k, bm, bn


def _project(x, w, col0, n, out_dtype, scale=1.0, name="proj"):
    m, k, bm, bn = _dense_blocks(x, n)
    return _dense_call(functools.partial(_mm_kernel, scale=scale), name, m, n, bm, bn,
                       [_row_spec(bm, k), _col_spec(k, bn, col0 // bn)], [out_dtype], (x, w))


def _project_kv(x, w, col0, n, name):
    m, k, bm, bn = _dense_blocks(x, n)
    return _dense_call(_mm2_kernel, name, m, n, bm, bn,
                       [_row_spec(bm, k), _col_spec(k, bn, col0 // bn)], [_F32, _BF16], (x, w))


def _project_glu(x, w, col0, c, name="glu"):
    m, k, bm, bn = _dense_blocks(x, c)
    return _dense_call(_glu_kernel, name, m, c, bm, bn,
                       [_row_spec(bm, k), _col_spec(k, bn, col0 // bn), _col_spec(k, bn, (col0 + c) // bn)],
                       [_F32], (x, w, w))


def _project_gate(x, w, col0, n, b_gate, name="gate"):
    m, k, bm, bn = _dense_blocks(x, n)
    b_spec = pl.BlockSpec((1, bn), lambda j, i: (0, j))
    return _dense_call(_gate_kernel, name, m, n, bm, bn,
                       [_row_spec(bm, k), _col_spec(k, bn, col0 // bn), b_spec], [_F32],
                       (x, w, b_gate.reshape(1, n)))


def _mix(att, cnv, w_att, w_cnv, gates, out_dtype, name="mix"):
    n = w_att.shape[1]
    m, ka, bm, bn = _dense_blocks(att, n)
    kc = cnv.shape[1]
    return _dense_call(_mix_kernel, name, m, n, bm, bn,
                       [_row_spec(bm, ka), _row_spec(bm, kc), _col_spec(ka, bn), _col_spec(kc, bn),
                        _tile_spec(bm, bn), _tile_spec(bm, bn, n // bn)],
                       [out_dtype], (att, cnv, w_att, w_cnv, gates, gates))


_LN_ROWS = 256


def _res_ln_kernel(x_ref, r_ref, g_ref, b_ref, *rest, alpha, n_blocks):
    o_ref = rest[-1]
    i = pl.program_id(0)

    @pl.when(i < n_blocks)
    def _():
        o_ref[...] = _layer_norm(alpha * x_ref[...] + r_ref[...], g_ref[...], b_ref[...])

    @pl.when(i >= n_blocks)
    def _():
        o_ref[...] = jnp.zeros_like(o_ref)


def _res_ln(x, r, g, b, alpha, total_rows, row0=0, into=None, name="res_ln"):
    m, d = x.shape
    bt = _blk(m, _LN_ROWS)
    assert row0 % bt == 0 and total_rows % bt == 0
    n_blocks = m // bt
    row = pl.BlockSpec((bt, d), lambda i: (jnp.minimum(i, n_blocks - 1), 0))
    vec = pl.BlockSpec((1, d), lambda i: (0, 0))
    args = [x, r, g.reshape(1, d), b.reshape(1, d)]
    in_specs = [row, row, vec, vec]
    aliases = {}
    grid = total_rows // bt
    if into is not None:
        assert into.shape == (total_rows, d)
        args.append(into)
        in_specs.append(pl.BlockSpec(memory_space=pl.ANY))
        aliases = {4: 0}
        grid = n_blocks
    else:
        assert row0 == 0
    return pl.pallas_call(
        functools.partial(_res_ln_kernel, alpha=alpha, n_blocks=n_blocks), grid=(grid,),
        in_specs=in_specs, out_specs=pl.BlockSpec((bt, d), lambda i: (i + row0 // bt, 0)),
        out_shape=jax.ShapeDtypeStruct((total_rows, d), _F32), input_output_aliases=aliases,
        compiler_params=_params("parallel"), name=name)(*args)


def _router_kernel(x_ref, w_ref, b_ref, o_ref, *, n_groups, per_group):
    logits = _dot(x_ref[...], w_ref[...]) + b_ref[...]
    lane = lax.broadcasted_iota(jnp.int32, logits.shape, 1).astype(_F32)
    neg = -jnp.inf
    big = 1e9

    def arg_max(v):
        m = jnp.max(v, axis=-1, keepdims=True)
        return m, jnp.min(jnp.where(v == m, lane, big), axis=-1, keepdims=True)

    glog = jnp.where(lane < n_groups, logits, neg)
    gmax, g_idx = arg_max(glog)
    g_p = 1.0 / jnp.sum(jnp.exp(glog - gmax), axis=-1, keepdims=True)
    lo = n_groups + g_idx * per_group
    elog = jnp.where((lane >= lo) & (lane < lo + per_group), logits, neg)
    m1, i1 = arg_max(elog)
    m2, i2 = arg_max(jnp.where(lane == i1, neg, elog))
    e2 = jnp.exp(m2 - m1)
    p1 = 1.0 / (1.0 + e2)
    p2 = e2 / (1.0 + e2)
    o_ref[...] = jnp.where(lane == 0, i1 - n_groups,
                 jnp.where(lane == 1, i2 - n_groups,
                 jnp.where(lane == 2, g_p * p1,
                 jnp.where(lane == 3, g_p * p2, 0.0))))


def _router(x1, row0, m, w_r, b_r, n_groups, per_group, name="router"):
    d = x1.shape[1]
    bt = _blk(m, 256)
    assert row0 % bt == 0
    return pl.pallas_call(
        functools.partial(_router_kernel, n_groups=n_groups, per_group=per_group), grid=(m // bt,),
        in_specs=[pl.BlockSpec((bt, d), lambda i: (i + row0 // bt, 0)),
                  pl.BlockSpec((d, _LANES), lambda i: (0, 0)),
                  pl.BlockSpec((1, _LANES), lambda i: (0, 0))],
        out_specs=pl.BlockSpec((bt, _LANES), lambda i: (i, 0)),
        out_shape=jax.ShapeDtypeStruct((m, _LANES), _F32),
        compiler_params=_params("parallel"), name=name)(x1, w_r, b_r)


_LOG2E = 1.4426950408889634
_SAMPLE_PAGES_PER_STEP = 4


def _neg_abs(z):
    return lax.bitcast_convert_type(lax.bitcast_convert_type(z, jnp.uint32) | jnp.uint32(0x80000000), _F32)


def _neg_softplus2(z):
    return jnp.maximum(z, 0.0) + jnp.log2(1.0 + jnp.exp2(_neg_abs(z)))


_MASKED_LOGIT = -1e30


def _neg_suffix_matrix(n, classes=1, copies=2):
    r = lax.broadcasted_iota(jnp.int32, (copies * n, n), 0) & (n - 1)
    c = lax.broadcasted_iota(jnp.int32, (copies * n, n), 1)
    keep = r >= c
    if classes > 1:
        keep = keep & ((r & (classes - 1)) == (c & (classes - 1)))
    return jnp.where(keep, -1.0, 0.0).astype(_BF16)


def _attn_prompt_kernel(bias_ref, q_ref, k_ref, v_ref, o_ref, acc_ref, carry_ref, z0_ref, z1_ref, s0_ref, s1_ref,
                        *, bq, sub):
    h = pl.program_id(0)
    qi = pl.program_id(1)
    bias = bias_ref[h]
    nsub = bq // sub
    neg_suffix = _neg_suffix_matrix(sub, copies=1)
    acc_ref[...] = jnp.zeros_like(acc_ref)
    carry_ref[...] = jnp.zeros_like(carry_ref)

    def front(j, z_ref, s_ref, masked=False):
        k0 = pl.multiple_of(j * bq, bq)
        z = lax.dot_general(q_ref[...], k_ref[pl.ds(k0, bq), :], (((1,), (1,)), ((), ())),
                            preferred_element_type=_F32) + bias
        if masked:
            q_pos = qi * bq + lax.broadcasted_iota(jnp.int32, (bq, bq), 0)
            k_pos = k0 + lax.broadcasted_iota(jnp.int32, (bq, bq), 1)
            z = jnp.where(k_pos < q_pos, z, _MASKED_LOGIT)
        z_ref[...] = z
        for s in range(nsub):
            s_ref[s * bq:(s + 1) * bq, :] = _neg_softplus2(z[:, s * sub:(s + 1) * sub]).astype(_BF16)

    def back(j, z_ref, s_ref):
        vb = v_ref[pl.ds(pl.multiple_of(j * bq, bq), bq), :]
        incl = jnp.dot(s_ref[...], neg_suffix, preferred_element_type=_F32)
        carry = carry_ref[...]
        ws = [None] * nsub
        for s in range(nsub - 1, -1, -1):
            inc = incl[s * bq:(s + 1) * bq, :]
            ws[s] = jnp.exp2(z_ref[:, s * sub:(s + 1) * sub] + inc + carry).astype(_BF16)
            carry = carry + inc[:, 0:1]
        carry_ref[...] = carry
        acc_ref[...] += jnp.dot(jnp.concatenate(ws, axis=1), vb, preferred_element_type=_F32)

    front(qi, z0_ref, s0_ref, masked=True)

    def pair(t, _):
        c = qi - 2 * t
        front(c - 1, z1_ref, s1_ref)
        back(c, z0_ref, s0_ref)
        front(c - 2, z0_ref, s0_ref)
        back(c - 1, z1_ref, s1_ref)
        return 0

    lax.fori_loop(0, qi // 2, pair, 0)

    @pl.when(qi % 2 == 1)
    def _():
        front(0, z1_ref, s1_ref)
        back(1, z0_ref, s0_ref)
        back(0, z1_ref, s1_ref)

    @pl.when(qi % 2 == 0)
    def _():
        back(0, z0_ref, s0_ref)

    o_ref[...] = acc_ref[...].astype(o_ref.dtype)


def _attn_prompt(q, k, v, bias, n_heads, name="attn_prompt"):
    t, a = q.shape
    dh = a // n_heads
    bq = _blk(t, 512)
    sub = _blk(bq, 256)
    assert sub & (sub - 1) == 0
    bias = bias * _LOG2E
    kernel = functools.partial(_attn_prompt_kernel, bq=bq, sub=sub)
    return pl.pallas_call(
        kernel, grid=(n_heads, t // bq),
        in_specs=[pl.BlockSpec(memory_space=pltpu.SMEM),
                  pl.BlockSpec((bq, dh), lambda h, i: (i, h)),
                  pl.BlockSpec((t, dh), lambda h, i: (0, h)),
                  pl.BlockSpec((t, dh), lambda h, i: (0, h))],
        out_specs=pl.BlockSpec((bq, dh), lambda h, i: (i, h)),
        out_shape=jax.ShapeDtypeStruct((t, a), _BF16),
        scratch_shapes=[pltpu.VMEM((bq, dh), _F32), pltpu.VMEM((bq, 1), _F32),
                        pltpu.VMEM((bq, bq), _F32), pltpu.VMEM((bq, bq), _F32),
                        pltpu.VMEM((bq * bq // sub, sub), _BF16), pltpu.VMEM((bq * bq // sub, sub), _BF16)],
        compiler_params=_params("parallel", "parallel"), name=name)(bias, q, k, v)


def _attn_sample_kernel(pt_ref, q_ref, bias_ref, *refs, n_heads, n_steps, pages):
    k_refs, v_refs = refs[:pages], refs[pages:2 * pages]
    o_ref, acc_ref, carry_ref = refs[2 * pages:]
    p = pl.program_id(1)
    hh = n_heads
    cols = k_refs[0].shape[0]
    nblk = cols // _LANES

    @pl.when(p == 0)
    def _():
        acc_ref[...] = jnp.zeros_like(acc_ref)
        carry_ref[...] = jnp.zeros_like(carry_ref)

    q = q_ref[...].astype(_BF16)
    bias = bias_ref[...]
    tot_r = lax.broadcasted_iota(jnp.int32, (2 * _LANES, _LANES), 0)
    tot_c = lax.broadcasted_iota(jnp.int32, (2 * _LANES, _LANES), 1)
    neg_total = jnp.where((tot_r & (hh - 1)) == (tot_c & (hh - 1)), -1.0, 0.0).astype(_BF16)
    sums_mat = jnp.concatenate([_neg_suffix_matrix(_LANES, classes=hh), neg_total], axis=1)
    row = lax.broadcasted_iota(jnp.int32, (hh, cols), 0)
    col = lax.broadcasted_iota(jnp.int32, (hh, cols), 1)
    real = (col & (hh - 1)) == row

    suffix = carry_ref[...]
    acc = acc_ref[...]
    for s in range(pages):
        kp = k_refs[s][...].astype(_BF16)
        vp = v_refs[s][...].astype(_BF16)
        z = lax.dot_general(q, kp, (((1,), (1,)), ((), ())), preferred_element_type=_F32) + bias
        sp = _neg_softplus2(z)
        stacked = jnp.concatenate([sp[:, i * _LANES:(i + 1) * _LANES] for i in range(nblk)], axis=0)
        hi, lo = _split_bf16(stacked)
        sums = jnp.dot(jnp.concatenate([hi, lo], axis=1), sums_mat, preferred_element_type=_F32)
        later = [None] * nblk
        for i in range(nblk - 1, -1, -1):
            later[i] = sums[i * hh:(i + 1) * hh, :_LANES] + suffix
            suffix = suffix + sums[i * hh:(i + 1) * hh, _LANES:]
        w = jnp.exp2(z + jnp.concatenate(later, axis=1))
        acc = acc + jnp.dot(jnp.where(real, w, 0.0).astype(_BF16), vp, preferred_element_type=_F32)
    carry_ref[...] = suffix
    acc_ref[...] = acc

    @pl.when(p == n_steps - 1)
    def _():
        o_ref[...] = acc


def _attn_sample(q, cache_k, cache_v, page_table, bias, layer, name="attn_sample"):
    b, hh, dh = q.shape
    assert hh & (hh - 1) == 0 and _LANES % hh == 0
    depth, n_phys, page = cache_k.shape[:3]
    n_pages = page_table.shape[1]
    ck = cache_k.reshape(depth, n_phys, page * hh, dh)
    cv = cache_v.reshape(depth, n_phys, page * hh, dh)
    pages = _SAMPLE_PAGES_PER_STEP if n_pages % _SAMPLE_PAGES_PER_STEP == 0 else 1
    n_steps = n_pages // pages

    def page_spec(s):
        return pl.BlockSpec((None, None, page * hh, dh),
                            lambda i, p, pt: (layer, pt[i * n_pages + (n_pages - 1 - (p * pages + s))], 0, 0))

    page_specs = [page_spec(s) for s in range(pages)]
    grid_spec = pltpu.PrefetchScalarGridSpec(
        num_scalar_prefetch=1, grid=(b, n_steps),
        in_specs=[pl.BlockSpec((None, hh, dh), lambda i, p, pt: (i, 0, 0)),
                  pl.BlockSpec((hh, 1), lambda i, p, pt: (0, 0))] + page_specs + page_specs,
        out_specs=pl.BlockSpec((None, hh, dh), lambda i, p, pt: (i, 0, 0)),
        scratch_shapes=[pltpu.VMEM((hh, dh), _F32), pltpu.VMEM((hh, _LANES), _F32)])
    return pl.pallas_call(
        functools.partial(_attn_sample_kernel, n_heads=hh, n_steps=n_steps, pages=pages),
        grid_spec=grid_spec, out_shape=jax.ShapeDtypeStruct((b, hh, dh), _F32),
        compiler_params=_params("parallel", "arbitrary"), name=name)(
            page_table.reshape(-1), q, (bias * _LOG2E).reshape(hh, 1), *([ck] * pages), *([cv] * pages))


_HALO = 32


def _conv_prompt_kernel(u_ref, prev_ref, hist_ref, w_ref, b_ref, g_ref, beta_ref, o_ref, win_ref, y_ref,
                        *, taps, tb, cw):
    i = pl.program_id(0)
    bt, ch = u_ref.shape
    lead = _HALO - (taps - 1)
    win_ref[0:_HALO, :] = jnp.where(i == 0, hist_ref[...], prev_ref[...])
    win_ref[_HALO:, :] = u_ref[...]
    by_residue = [[k for k in range(taps) if (k + lead) % _SUBLANES == res] for res in range(_SUBLANES)]
    for c0 in range(0, ch, cw):
        for t0 in range(0, bt, tb):
            acc = jnp.zeros((tb // _SUBLANES, _SUBLANES, cw), _F32)
            for res, ks in enumerate(by_residue):
                if not ks:
                    continue
                reach = ks[-1] + lead - res
                shifted = win_ref[t0 + res:t0 + res + tb + reach, c0:c0 + cw]
                shifted = shifted.reshape((tb + reach) // _SUBLANES, _SUBLANES, cw)
                for k in ks:
                    a0 = (k + lead - res) // _SUBLANES
                    acc = acc + shifted[a0:a0 + tb // _SUBLANES] * w_ref[k, :, c0:c0 + cw][None]
            y_ref[t0:t0 + tb, c0:c0 + cw] = acc.reshape(tb, cw)
    y = _layer_norm(y_ref[...] + b_ref[...], g_ref[...], beta_ref[...])
    o_ref[...] = (y * _sigmoid(y)).astype(o_ref.dtype)


def _conv_prompt(u, hist, conv_w, conv_b, ln_g, ln_b, out_dtype, name="conv_prompt"):
    t, ch = u.shape
    taps = conv_w.shape[0]
    assert taps - 1 <= _HALO and t % _HALO == 0
    bt = _blk(t, 128)
    hist_pad = jnp.concatenate([jnp.zeros((_HALO - (taps - 1), ch), u.dtype), hist.astype(u.dtype)], axis=0)
    ratio = bt // _HALO
    vec = pl.BlockSpec((1, ch), lambda i: (0, 0))
    kernel = functools.partial(_conv_prompt_kernel, taps=taps, tb=min(bt, 64), cw=min(ch, 256))
    return pl.pallas_call(
        kernel, grid=(t // bt,),
        in_specs=[pl.BlockSpec((bt, ch), lambda i: (i, 0)),
                  pl.BlockSpec((_HALO, ch), lambda i: (jnp.maximum(i * ratio - 1, 0), 0)),
                  pl.BlockSpec((_HALO, ch), lambda i: (0, 0)),
                  pl.BlockSpec((taps, _SUBLANES, ch), lambda i: (0, 0, 0)), vec, vec, vec],
        out_specs=pl.BlockSpec((bt, ch), lambda i: (i, 0)),
        out_shape=jax.ShapeDtypeStruct((t, ch), out_dtype),
        scratch_shapes=[pltpu.VMEM((bt + _HALO, ch), _F32), pltpu.VMEM((bt, ch), _F32)],
        compiler_params=_params("parallel"), name=name)(
            u, u, hist_pad, jnp.broadcast_to(conv_w[:, None, :], (taps, _SUBLANES, ch)),
            conv_b.reshape(1, ch), ln_g.reshape(1, ch), ln_b.reshape(1, ch))


def _conv_sample_kernel(u_ref, s_ref, w_ref, b_ref, g_ref, beta_ref, o_ref, ns_ref, *, taps):
    s = s_ref[...]
    u = u_ref[...]
    conv = jnp.sum(s * w_ref[0:taps - 1, :], axis=0, keepdims=True) + u * w_ref[taps - 1:taps, :]
    y = _layer_norm(conv + b_ref[...], g_ref[...], beta_ref[...])
    o_ref[...] = y * _sigmoid(y)
    ns_ref[0:taps - 2, :] = s_ref[1:taps - 1, :]
    ns_ref[taps - 2:taps - 1, :] = u


def _conv_sample(u, state, conv_w, conv_b, ln_g, ln_b, name="conv_sample"):
    b, ch = u.shape
    taps = conv_w.shape[0]
    vec = pl.BlockSpec((1, ch), lambda i: (0, 0))
    row = pl.BlockSpec((None, 1, ch), lambda i: (i, 0, 0))
    st = pl.BlockSpec((None, taps - 1, ch), lambda i: (i, 0, 0))
    out, new_state = pl.pallas_call(
        functools.partial(_conv_sample_kernel, taps=taps), grid=(b,),
        in_specs=[row, st, pl.BlockSpec((taps, ch), lambda i: (0, 0)), vec, vec, vec],
        out_specs=[row, st],
        out_shape=[jax.ShapeDtypeStruct((b, 1, ch), _F32), jax.ShapeDtypeStruct((b, taps - 1, ch), _F32)],
        compiler_params=_params("parallel"), name=name)(
            u.reshape(b, 1, ch), state, conv_w, conv_b.reshape(1, ch), ln_g.reshape(1, ch), ln_b.reshape(1, ch))
    return out.reshape(b, ch), new_state


_DMA_UNROLL = 8


def _row_copy(src_hbm, src_row, dst_ref, dst_row, sem):
    return pltpu.make_async_copy(src_hbm.at[pl.ds(src_row, 1), :], dst_ref.at[pl.ds(dst_row, 1), :], sem)


def _moe_up_kernel(be_ref, na_ref, tok_ref, x_hbm, w1_ref, w3_ref, o_ref, xbuf_ref, xb_ref, sem, *, rows, kc):
    i = pl.program_id(1)
    n_active = na_ref[0]
    slot = i % 2
    d = xb_ref.shape[1]

    def start(blk, s):
        def body(r, _):
            _row_copy(x_hbm, tok_ref[blk * rows + r], xbuf_ref.at[s], r, sem.at[s]).start()
            return 0
        lax.fori_loop(0, rows, body, 0, unroll=_DMA_UNROLL)

    @pl.when((i == 0) & (n_active > 0))
    def _():
        start(0, 0)

    @pl.when(i + 1 < n_active)
    def _():
        start(i + 1, 1 - slot)

    @pl.when(i < n_active)
    def _():
        def drain(r, _):
            _row_copy(x_hbm, 0, xbuf_ref.at[slot], r, sem.at[slot]).wait()
            return 0
        lax.fori_loop(0, rows, drain, 0, unroll=_DMA_UNROLL)
        xb_ref[...] = xbuf_ref[slot].astype(_BF16)
        a = jnp.zeros(o_ref.shape, _F32)
        b = jnp.zeros(o_ref.shape, _F32)
        for c0 in range(0, d, kc):
            x = xb_ref[:, c0:c0 + kc]
            a = a + _dot(x, w1_ref[c0:c0 + kc, :].astype(_BF16))
            b = b + _dot(x, w3_ref[c0:c0 + kc, :].astype(_BF16))
        o_ref[...] = (a * _sigmoid(a) * b).astype(o_ref.dtype)

    @pl.when(i >= n_active)
    def _():
        o_ref[...] = jnp.zeros_like(o_ref)


def _expert_changed(be_ref, i):
    prev = be_ref[jnp.maximum(i - 1, 0)]
    return (i == 0) | (be_ref[i] != prev)


def _gmm2_kernel(be_ref, na_ref, h_ref, w2_ref, o_ref, w2b_ref):
    i = pl.program_id(1)

    @pl.when(i < na_ref[0])
    def _():
        @pl.when(_expert_changed(be_ref, i))
        def _():
            w2b_ref[...] = w2_ref[...].astype(_BF16)

        o_ref[...] = _dot(h_ref[...], w2b_ref[...])

    @pl.when(i >= na_ref[0])
    def _():
        o_ref[...] = jnp.zeros_like(o_ref)


def _expert_ffn(x, row_tok, block_e, n_active, w1, w3, w2, layer):
    n_rows = row_tok.shape[0]
    d = x.shape[1]
    f = w1.shape[-1]
    bm = _MOE_ROWS
    tn1 = _blk(f, 512)
    tn2 = _blk(d, 2048)
    w_in_spec = pl.BlockSpec((None, None, d, tn1), lambda j, i, be, na, tok: (layer, be[i], 0, j))
    hidden = pl.pallas_call(
        functools.partial(_moe_up_kernel, rows=bm, kc=_blk(d, 512)),
        grid_spec=pltpu.PrefetchScalarGridSpec(
            num_scalar_prefetch=3, grid=(f // tn1, n_rows // bm),
            in_specs=[pl.BlockSpec(memory_space=pl.ANY), w_in_spec, w_in_spec],
            out_specs=pl.BlockSpec((bm, tn1), lambda j, i, be, na, tok: (i, j)),
            scratch_shapes=[pltpu.VMEM((2, bm, d), x.dtype), pltpu.VMEM((bm, d), _BF16),
                            pltpu.SemaphoreType.DMA((2,))]),
        out_shape=jax.ShapeDtypeStruct((n_rows, f), _BF16),
        compiler_params=_params("arbitrary", "arbitrary"), name="moe_up")(
            block_e, n_active, row_tok, x, w1, w3)

    def last_active(i, na):
        return jnp.minimum(i, jnp.maximum(na[0] - 1, 0))

    return pl.pallas_call(
        _gmm2_kernel,
        grid_spec=pltpu.PrefetchScalarGridSpec(
            num_scalar_prefetch=2, grid=(d // tn2, n_rows // bm),
            in_specs=[pl.BlockSpec((bm, f), lambda j, i, be, na: (last_active(i, na), 0)),
                      pl.BlockSpec((None, None, f, tn2), lambda j, i, be, na: (layer, be[i], 0, j))],
            out_specs=pl.BlockSpec((bm, tn2), lambda j, i, be, na: (i, j)),
            scratch_shapes=[pltpu.VMEM((f, tn2), _BF16)]),
        out_shape=jax.ShapeDtypeStruct((n_rows, d), _F32),
        compiler_params=_params("arbitrary", "arbitrary"), name="moe_down")(block_e, n_active, hidden, w2)


def _combine_kernel(pos_ref, x_ref, r_ref, g_ref, b_ref, ys_hbm, o_ref, buf_ref, sem, *, bt, alpha):
    i = pl.program_id(0)
    slot = i % 2

    def start(blk, s):
        def body(r, _):
            for k in range(TOP_K):
                _row_copy(ys_hbm, pos_ref[(blk * bt + r) * TOP_K + k], buf_ref.at[s], k * bt + r, sem.at[s]).start()
            return 0
        lax.fori_loop(0, bt, body, 0, unroll=_DMA_UNROLL)

    @pl.when(i == 0)
    def _():
        start(0, 0)

    @pl.when(i + 1 < pl.num_programs(0))
    def _():
        start(i + 1, 1 - slot)

    def drain(r, _):
        _row_copy(ys_hbm, 0, buf_ref.at[slot], r, sem.at[slot]).wait()
        return 0

    lax.fori_loop(0, TOP_K * bt, drain, 0, unroll=_DMA_UNROLL)
    slab = r_ref[...]
    ffn = buf_ref[slot, 0:bt, :] * slab[:, 2:3]
    for k in range(1, TOP_K):
        ffn = ffn + buf_ref[slot, k * bt:(k + 1) * bt, :] * slab[:, 2 + k:3 + k]
    o_ref[...] = _layer_norm(alpha * x_ref[...] + ffn, g_ref[...], b_ref[...])


def _combine_ln(x1, row0, m, slab, pos, ys, g, b, alpha, name="moe_combine"):
    d = x1.shape[1]
    bt = _blk(m, 128)
    assert row0 % bt == 0
    vec = pl.BlockSpec((1, d), lambda i, pos: (0, 0))
    grid_spec = pltpu.PrefetchScalarGridSpec(
        num_scalar_prefetch=1, grid=(m // bt,),
        in_specs=[pl.BlockSpec((bt, d), lambda i, pos: (i + row0 // bt, 0)),
                  pl.BlockSpec((bt, _LANES), lambda i, pos: (i, 0)), vec, vec,
                  pl.BlockSpec(memory_space=pl.ANY)],
        out_specs=pl.BlockSpec((bt, d), lambda i, pos: (i, 0)),
        scratch_shapes=[pltpu.VMEM((2, TOP_K * bt, d), _F32), pltpu.SemaphoreType.DMA((2,))])
    return pl.pallas_call(
        functools.partial(_combine_kernel, bt=bt, alpha=alpha), grid_spec=grid_spec,
        out_shape=jax.ShapeDtypeStruct((m, d), _F32),
        compiler_params=_params("arbitrary"), name=name)(pos, x1, slab, g.reshape(1, d), b.reshape(1, d), ys)


def _dispatch_plan(expert, n_experts, bm):
    a = expert.shape[0]
    n_blocks = -(-(a + n_experts * (bm - 1)) // bm)
    n_rows = n_blocks * bm
    onehot = (expert[:, None] == jnp.arange(n_experts, dtype=jnp.int32)[None, :]).astype(jnp.int32)
    rank = jnp.take_along_axis(jnp.cumsum(onehot, axis=0) - onehot, expert[:, None], axis=1)[:, 0]
    counts = jnp.sum(onehot, axis=0)
    padded = (counts + bm - 1) // bm * bm
    pad_end = jnp.cumsum(padded)
    pad_start = pad_end - padded
    dest = (pad_start[expert] + rank).astype(jnp.int32)
    row_src = jnp.full((n_rows,), -1, jnp.int32).at[dest].set(jnp.arange(a, dtype=jnp.int32))
    n_active = (pad_end[-1] // bm).astype(jnp.int32)
    blk = jnp.minimum(jnp.arange(n_blocks, dtype=jnp.int32), jnp.maximum(n_active - 1, 0))
    block_e = jnp.minimum(jnp.searchsorted(pad_end, blk * bm, side="right"), n_experts - 1).astype(jnp.int32)
    return dest, row_src, block_e, n_active.reshape(1)


def _branches(x, w_in, w_att_out, w_conv_out, w_out, b_gate, attn_fn, conv_fn, widths, kv_bf16):
    a, c, d = widths
    scale = float(a // attn_fn.n_heads) ** -0.5 * _LOG2E
    act = x.dtype
    q = _project(x, w_in, 0, a, act, scale=scale, name="proj_q")
    if kv_bf16:
        k, k_b = _project_kv(x, w_in, a, a, "proj_k")
        v, v_b = _project_kv(x, w_in, 2 * a, a, "proj_v")
    else:
        k = k_b = _project(x, w_in, a, a, _F32, name="proj_k")
        v = v_b = _project(x, w_in, 2 * a, a, _F32, name="proj_v")
    u = _project_glu(x, w_in, 3 * a, c)
    gates = _project_gate(x, w_in, 3 * a + 2 * c, 2 * d, b_gate)
    att = attn_fn(q, k_b, v_b)
    cnv, conv_state = conv_fn(u)
    pre = _mix(att.astype(act), cnv.astype(act), w_att_out, w_conv_out, gates, act)
    mixed = _project(pre, w_out, 0, d, _F32, name="proj_out")
    return mixed, k, v, conv_state


class _PromptAttn:
    def __init__(self, bias, n_heads):
        self.bias, self.n_heads = bias, n_heads

    def __call__(self, q, k, v):
        return _attn_prompt(q, k, v, self.bias, self.n_heads)


class _SampleAttn:
    def __init__(self, bias, n_heads, cache_k, cache_v, page_table, layer):
        self.bias, self.n_heads = bias, n_heads
        self.cache_k, self.cache_v, self.page_table, self.layer = cache_k, cache_v, page_table, layer

    def __call__(self, q, k, v):
        b, a = q.shape
        out = _attn_sample(q.reshape(b, self.n_heads, a // self.n_heads), self.cache_k, self.cache_v,
                           self.page_table, self.bias, self.layer)
        return out.reshape(b, a)


def kernel(x_prompt, x_sample, cache_k, cache_v, state_conv, page_table, w_in, b_sb, b_gate, conv_w, conv_b,
           conv_ln_g, conv_ln_b, w_att_out, w_conv_out, w_out, ln1_g, ln1_b, w_router_group, b_router_group,
           w_router_expert, b_router_expert, w1, w3, w2, ln2_g, ln2_b):
    depth, d, _ = w_in.shape
    n_heads, d_head = cache_k.shape[3], cache_k.shape[4]
    a = n_heads * d_head
    c = conv_w.shape[2]
    taps = conv_w.shape[1]
    n_groups = w_router_group.shape[2]
    n_experts = w_router_expert.shape[2]
    per_group = n_experts // n_groups
    assert n_groups + n_experts <= _LANES
    alpha = (2.0 * depth) ** 0.25
    bsz, seq, _ = x_prompt.shape
    dec, dec_seq, _ = x_sample.shape
    assert bsz == 1 and dec_seq == 1
    n_p, n_s = bsz * seq, dec * dec_seq

    y_p = x_prompt.reshape(n_p, d)
    y_s = x_sample.reshape(n_s, d)
    outs = {k: [] for k in ("kp", "vp", "cp", "ks", "vs", "cs")}
    for l in range(depth):
        win, wao, wco, wo = w_in[l], w_att_out[l], w_conv_out[l], w_out[l]
        hist0 = jnp.zeros((taps - 1, c), _F32)
        conv_p = lambda u, l=l, hist0=hist0: (
            _conv_prompt(u, hist0, conv_w[l], conv_b[l], conv_ln_g[l], conv_ln_b[l], _BF16), u[seq - (taps - 1):])
        weights = (win.astype(_BF16), wao.astype(_BF16), wco.astype(_BF16), wo.astype(_BF16))
        mixed_p, k_p, v_p, cs_p = _branches(
            y_p.astype(_BF16), *weights, b_gate[l], _PromptAttn(b_sb[l], n_heads), conv_p, (a, c, d), True)
        conv_s = lambda u, l=l: _conv_sample(u, state_conv[l], conv_w[l], conv_b[l], conv_ln_g[l], conv_ln_b[l])
        mixed_s, k_s, v_s, cs_s = _branches(
            y_s.astype(_BF16), *weights, b_gate[l],
            _SampleAttn(b_sb[l], n_heads, cache_k, cache_v, page_table, l), conv_s, (a, c, d), False)

        n_all = n_p + -(-n_s // _LN_ROWS) * _LN_ROWS
        x1 = _res_ln(y_p, mixed_p, ln1_g[l], ln1_b[l], alpha, n_all, name="ln1_prompt")
        x1 = _res_ln(y_s, mixed_s, ln1_g[l], ln1_b[l], alpha, n_all, row0=n_p, into=x1, name="ln1_sample")

        w_r = jnp.zeros((d, _LANES), _F32).at[:, :n_groups].set(w_router_group[l]) \
                 .at[:, n_groups:n_groups + n_experts].set(w_router_expert[l])
        b_r = jnp.zeros((1, _LANES), _F32).at[0, :n_groups].set(b_router_group[l]) \
                 .at[0, n_groups:n_groups + n_experts].set(b_router_expert[l])
        slab_p = _router(x1, 0, n_p, w_r, b_r, n_groups, per_group, name="router_prompt")
        slab_s = _router(x1, n_p, n_s, w_r, b_r, n_groups, per_group, name="router_sample")
        expert = jnp.concatenate([slab_p[:, :TOP_K], slab_s[:, :TOP_K]], axis=0).astype(jnp.int32).reshape(-1)
        dest, row_src, block_e, n_active = _dispatch_plan(expert, n_experts, _MOE_ROWS)
        row_tok = jnp.maximum(row_src, 0) // TOP_K
        ys = _expert_ffn(x1, row_tok, block_e, n_active, w1, w3, w2, l)
        y_p = _combine_ln(x1, 0, n_p, slab_p, dest[:n_p * TOP_K], ys, ln2_g[l], ln2_b[l], alpha,
                          name="combine_prompt")
        y_s = _combine_ln(x1, n_p, n_s, slab_s, dest[n_p * TOP_K:], ys, ln2_g[l], ln2_b[l], alpha,
                          name="combine_sample")

        outs["kp"].append(k_p.reshape(bsz, seq, n_heads, d_head))
        outs["vp"].append(v_p.reshape(bsz, seq, n_heads, d_head))
        outs["cp"].append(cs_p.reshape(bsz, taps - 1, c))
        outs["ks"].append(k_s.reshape(dec, dec_seq, n_heads, d_head))
        outs["vs"].append(v_s.reshape(dec, dec_seq, n_heads, d_head))
        outs["cs"].append(cs_s)
    return (y_p.reshape(bsz, seq, d), y_s.reshape(dec, dec_seq, d),
            jnp.stack(outs["kp"]), jnp.stack(outs["vp"]), jnp.stack(outs["cp"]),
            jnp.stack(outs["ks"]), jnp.stack(outs["vs"]), jnp.stack(outs["cs"]))
```

```python
import functools

import jax
import jax.numpy as jnp
from jax import lax
from jax.experimental import pallas as pl
from jax.experimental.pallas import tpu as pltpu

_F32 = jnp.float32
_BF16 = jnp.bfloat16

LN_EPS = 1e-5
TOP_K = 2
_VMEM_LIMIT_V7X = 56 * 1024 * 1024
_LANES = 128
_SUBLANES = 8
_MOE_ROWS = 256


def _params(*sem):
    return pltpu.CompilerParams(dimension_semantics=sem, vmem_limit_bytes=_VMEM_LIMIT_V7X)


def _blk(dim, pref):
    if dim <= pref:
        return dim
    assert dim % pref == 0, (dim, pref)
    return pref


def _split_bf16(x):
    hi = x.astype(_BF16)
    lo = (x - hi.astype(_F32)).astype(_BF16)
    return hi, lo


def _dot(x, w):
    assert x.dtype == w.dtype, (x.dtype, w.dtype)
    d = functools.partial(jnp.dot, preferred_element_type=_F32)
    if x.dtype == _BF16:
        return d(x, w)
    xh, xl = _split_bf16(x)
    wh, wl = _split_bf16(w)
    return d(xh, wh) + (d(xl, wh) + d(xh, wl))


def _sigmoid(x):
    return 1.0 / (1.0 + jnp.exp(-x))


def _layer_norm(v, g, b):
    mu = jnp.mean(v, axis=-1, keepdims=True)
    d = v - mu
    var = jnp.mean(d * d, axis=-1, keepdims=True)
    return d * lax.rsqrt(var + LN_EPS) * g + b


def _mm_kernel(x_ref, w_ref, o_ref, *, scale):
    acc = _dot(x_ref[...], w_ref[...])
    if scale != 1.0:
        acc = acc * scale
    o_ref[...] = acc.astype(o_ref.dtype)


def _mm2_kernel(x_ref, w_ref, o_ref, ob_ref):
    acc = _dot(x_ref[...], w_ref[...])
    o_ref[...] = acc
    ob_ref[...] = acc.astype(ob_ref.dtype)


def _glu_kernel(x_ref, wv_ref, wg_ref, o_ref):
    x = x_ref[...]
    o_ref[...] = _dot(x, wv_ref[...]) * _sigmoid(_dot(x, wg_ref[...]))


def _gate_kernel(x_ref, w_ref, b_ref, o_ref):
    o_ref[...] = _sigmoid(_dot(x_ref[...], w_ref[...]) + b_ref[...])


def _mix_kernel(a_ref, c_ref, wa_ref, wc_ref, ga_ref, gc_ref, o_ref):
    att = _dot(a_ref[...], wa_ref[...])
    cnv = _dot(c_ref[...], wc_ref[...])
    o_ref[...] = (ga_ref[...] * att + gc_ref[...] * cnv).astype(o_ref.dtype)


def _row_spec(bm, k):
    return pl.BlockSpec((bm, k), lambda j, i: (i, 0))


def _col_spec(k, bn, off_blocks=0):
    return pl.BlockSpec((k, bn), lambda j, i: (0, j + off_blocks))


def _tile_spec(bm, bn, off_blocks=0):
    return pl.BlockSpec((bm, bn), lambda j, i: (i, j + off_blocks))


def _dense_call(kernel, name, m, n, bm, bn, in_specs, out_dtypes, args):
    outs = [jax.ShapeDtypeStruct((m, n), dt) for dt in out_dtypes]
    specs = [_tile_spec(bm, bn) for _ in out_dtypes]
    single = len(outs) == 1
    return pl.pallas_call(
        kernel, grid=(n // bn, m // bm), in_specs=in_specs,
        out_specs=specs[0] if single else specs,
        out_shape=outs[0] if single else outs,
        compiler_params=_params("parallel", "parallel"), name=name)(*args)


def _dense_blocks(x, n):
    m, k = x.shape
    prec3 = x.dtype == _F32
    bm = _blk(m, 512)
    bn = _blk(n, 256 if prec3 else 1024)
    return m, k, bm, bn


def _project(x, w, col0, n, out_dtype, scale=1.0, name="proj"):
    m, k, bm, bn = _dense_blocks(x, n)
    return _dense_call(functools.partial(_mm_kernel, scale=scale), name, m, n, bm, bn,
                       [_row_spec(bm, k), _col_spec(k, bn, col0 // bn)], [out_dtype], (x, w))


def _project_kv(x, w, col0, n, name):
    m, k, bm, bn = _dense_blocks(x, n)
    return _dense_call(_mm2_kernel, name, m, n, bm, bn,
                       [_row_spec(bm, k), _col_spec(k, bn, col0 // bn)], [_F32, _BF16], (x, w))


def _project_glu(x, w, col0, c, name="glu"):
    m, k, bm, bn = _dense_blocks(x, c)
    return _dense_call(_glu_kernel, name, m, c, bm, bn,
                       [_row_spec(bm, k), _col_spec(k, bn, col0 // bn), _col_spec(k, bn, (col0 + c) // bn)],
                       [_F32], (x, w, w))


def _project_gate(x, w, col0, n, b_gate, name="gate"):
    m, k, bm, bn = _dense_blocks(x, n)
    b_spec = pl.BlockSpec((1, bn), lambda j, i: (0, j))
    return _dense_call(_gate_kernel, name, m, n, bm, bn,
                       [_row_spec(bm, k), _col_spec(k, bn, col0 // bn), b_spec], [_F32],
                       (x, w, b_gate.reshape(1, n)))


def _mix(att, cnv, w_att, w_cnv, gates, out_dtype, name="mix"):
    n = w_att.shape[1]
    m, ka, bm, bn = _dense_blocks(att, n)
    kc = cnv.shape[1]
    return _dense_call(_mix_kernel, name, m, n, bm, bn,
                       [_row_spec(bm, ka), _row_spec(bm, kc), _col_spec(ka, bn), _col_spec(kc, bn),
                        _tile_spec(bm, bn), _tile_spec(bm, bn, n // bn)],
                       [out_dtype], (att, cnv, w_att, w_cnv, gates, gates))


_LN_ROWS = 256


def _res_ln_kernel(x_ref, r_ref, g_ref, b_ref, *rest, alpha, n_blocks):
    o_ref = rest[-1]
    i = pl.program_id(0)

    @pl.when(i < n_blocks)
    def _():
        o_ref[...] = _layer_norm(alpha * x_ref[...] + r_ref[...], g_ref[...], b_ref[...])

    @pl.when(i >= n_blocks)
    def _():
        o_ref[...] = jnp.zeros_like(o_ref)


def _res_ln(x, r, g, b, alpha, total_rows, row0=0, into=None, name="res_ln"):
    m, d = x.shape
    bt = _blk(m, _LN_ROWS)
    assert row0 % bt == 0 and total_rows % bt == 0
    n_blocks = m // bt
    row = pl.BlockSpec((bt, d), lambda i: (jnp.minimum(i, n_blocks - 1), 0))
    vec = pl.BlockSpec((1, d), lambda i: (0, 0))
    args = [x, r, g.reshape(1, d), b.reshape(1, d)]
    in_specs = [row, row, vec, vec]
    aliases = {}
    grid = total_rows // bt
    if into is not None:
        assert into.shape == (total_rows, d)
        args.append(into)
        in_specs.append(pl.BlockSpec(memory_space=pl.ANY))
        aliases = {4: 0}
        grid = n_blocks
    else:
        assert row0 == 0
    return pl.pallas_call(
        functools.partial(_res_ln_kernel, alpha=alpha, n_blocks=n_blocks), grid=(grid,),
        in_specs=in_specs, out_specs=pl.BlockSpec((bt, d), lambda i: (i + row0 // bt, 0)),
        out_shape=jax.ShapeDtypeStruct((total_rows, d), _F32), input_output_aliases=aliases,
        compiler_params=_params("parallel"), name=name)(*args)


def _router_kernel(x_ref, w_ref, b_ref, o_ref, *, n_groups, per_group):
    logits = _dot(x_ref[...], w_ref[...]) + b_ref[...]
    lane = lax.broadcasted_iota(jnp.int32, logits.shape, 1).astype(_F32)
    neg = -jnp.inf
    big = 1e9

    def arg_max(v):
        m = jnp.max(v, axis=-1, keepdims=True)
        return m, jnp.min(jnp.where(v == m, lane, big), axis=-1, keepdims=True)

    glog = jnp.where(lane < n_groups, logits, neg)
    gmax, g_idx = arg_max(glog)
    g_p = 1.0 / jnp.sum(jnp.exp(glog - gmax), axis=-1, keepdims=True)
    lo = n_groups + g_idx * per_group
    elog = jnp.where((lane >= lo) & (lane < lo + per_group), logits, neg)
    m1, i1 = arg_max(elog)
    m2, i2 = arg_max(jnp.where(lane == i1, neg, elog))
    e2 = jnp.exp(m2 - m1)
    p1 = 1.0 / (1.0 + e2)
    p2 = e2 / (1.0 + e2)
    o_ref[...] = jnp.where(lane == 0, i1 - n_groups,
                 jnp.where(lane == 1, i2 - n_groups,
                 jnp.where(lane == 2, g_p * p1,
                 jnp.where(lane == 3, g_p * p2, 0.0))))


def _router(x1, row0, m, w_r, b_r, n_groups, per_group, name="router"):
    d = x1.shape[1]
    bt = _blk(m, 256)
    assert row0 % bt == 0
    return pl.pallas_call(
        functools.partial(_router_kernel, n_groups=n_groups, per_group=per_group), grid=(m // bt,),
        in_specs=[pl.BlockSpec((bt, d), lambda i: (i + row0 // bt, 0)),
                  pl.BlockSpec((d, _LANES), lambda i: (0, 0)),
                  pl.BlockSpec((1, _LANES), lambda i: (0, 0))],
        out_specs=pl.BlockSpec((bt, _LANES), lambda i: (i, 0)),
        out_shape=jax.ShapeDtypeStruct((m, _LANES), _F32),
        compiler_params=_params("parallel"), name=name)(x1, w_r, b_r)


_LOG2E = 1.4426950408889634
_SAMPLE_PAGES_PER_STEP = 8


def _neg_abs(z):
    return lax.bitcast_convert_type(lax.bitcast_convert_type(z, jnp.uint32) | jnp.uint32(0x80000000), _F32)


def _neg_softplus2(z):
    return jnp.maximum(z, 0.0) + jnp.log2(1.0 + jnp.exp2(_neg_abs(z)))


_MASKED_LOGIT = -1e30


def _neg_suffix_matrix(n, classes=1, copies=2):
    r = lax.broadcasted_iota(jnp.int32, (copies * n, n), 0) & (n - 1)
    c = lax.broadcasted_iota(jnp.int32, (copies * n, n), 1)
    keep = r >= c
    if classes > 1:
        keep = keep & ((r & (classes - 1)) == (c & (classes - 1)))
    return jnp.where(keep, -1.0, 0.0).astype(_BF16)


def _attn_prompt_kernel(bias_ref, q_ref, k_ref, v_ref, o_ref, acc_ref, carry_ref, z0_ref, z1_ref, s0_ref, s1_ref,
                        *, bq, sub):
    h = pl.program_id(0)
    qi = pl.program_id(1)
    bias = bias_ref[h]
    nsub = bq // sub
    neg_suffix = _neg_suffix_matrix(sub, copies=1)
    acc_ref[...] = jnp.zeros_like(acc_ref)
    carry_ref[...] = jnp.zeros_like(carry_ref)

    def front(j, z_ref, s_ref, masked=False):
        k0 = pl.multiple_of(j * bq, bq)
        z = lax.dot_general(q_ref[...], k_ref[pl.ds(k0, bq), :], (((1,), (1,)), ((), ())),
                            preferred_element_type=_F32) + bias
        if masked:
            q_pos = qi * bq + lax.broadcasted_iota(jnp.int32, (bq, bq), 0)
            k_pos = k0 + lax.broadcasted_iota(jnp.int32, (bq, bq), 1)
            z = jnp.where(k_pos < q_pos, z, _MASKED_LOGIT)
        z_ref[...] = z
        for s in range(nsub):
            s_ref[s * bq:(s + 1) * bq, :] = _neg_softplus2(z[:, s * sub:(s + 1) * sub]).astype(_BF16)

    def back(j, z_ref, s_ref):
        vb = v_ref[pl.ds(pl.multiple_of(j * bq, bq), bq), :]
        incl = jnp.dot(s_ref[...], neg_suffix, preferred_element_type=_F32)
        carry = carry_ref[...]
        ws = [None] * nsub
        for s in range(nsub - 1, -1, -1):
            inc = incl[s * bq:(s + 1) * bq, :]
            ws[s] = jnp.exp2(z_ref[:, s * sub:(s + 1) * sub] + inc + carry).astype(_BF16)
            carry = carry + inc[:, 0:1]
        carry_ref[...] = carry
        acc_ref[...] += jnp.dot(jnp.concatenate(ws, axis=1), vb, preferred_element_type=_F32)

    front(qi, z0_ref, s0_ref, masked=True)

    def pair(t, _):
        c = qi - 2 * t
        front(c - 1, z1_ref, s1_ref)
        back(c, z0_ref, s0_ref)
        front(c - 2, z0_ref, s0_ref)
        back(c - 1, z1_ref, s1_ref)
        return 0

    lax.fori_loop(0, qi // 2, pair, 0)

    @pl.when(qi % 2 == 1)
    def _():
        front(0, z1_ref, s1_ref)
        back(1, z0_ref, s0_ref)
        back(0, z1_ref, s1_ref)

    @pl.when(qi % 2 == 0)
    def _():
        back(0, z0_ref, s0_ref)

    o_ref[...] = acc_ref[...].astype(o_ref.dtype)


def _attn_prompt(q, k, v, bias, n_heads, name="attn_prompt"):
    t, a = q.shape
    dh = a // n_heads
    bq = _blk(t, 512)
    sub = _blk(bq, 256)
    assert sub & (sub - 1) == 0
    bias = bias * _LOG2E
    kernel = functools.partial(_attn_prompt_kernel, bq=bq, sub=sub)
    return pl.pallas_call(
        kernel, grid=(n_heads, t // bq),
        in_specs=[pl.BlockSpec(memory_space=pltpu.SMEM),
                  pl.BlockSpec((bq, dh), lambda h, i: (i, h)),
                  pl.BlockSpec((t, dh), lambda h, i: (0, h)),
                  pl.BlockSpec((t, dh), lambda h, i: (0, h))],
        out_specs=pl.BlockSpec((bq, dh), lambda h, i: (i, h)),
        out_shape=jax.ShapeDtypeStruct((t, a), _BF16),
        scratch_shapes=[pltpu.VMEM((bq, dh), _F32), pltpu.VMEM((bq, 1), _F32),
                        pltpu.VMEM((bq, bq), _F32), pltpu.VMEM((bq, bq), _F32),
                        pltpu.VMEM((bq * bq // sub, sub), _BF16), pltpu.VMEM((bq * bq // sub, sub), _BF16)],
        compiler_params=_params("parallel", "parallel"), name=name)(bias, q, k, v)


def _attn_sample_kernel(pt_ref, q_ref, bias_ref, *refs, n_heads, n_steps, pages):
    k_refs, v_refs = refs[:pages], refs[pages:2 * pages]
    o_ref, acc_ref, carry_ref = refs[2 * pages:]
    p = pl.program_id(1)
    hh = n_heads
    cols = k_refs[0].shape[0]
    nblk = cols // _LANES

    @pl.when(p == 0)
    def _():
        acc_ref[...] = jnp.zeros_like(acc_ref)
        carry_ref[...] = jnp.zeros_like(carry_ref)

    q = q_ref[...].astype(_BF16)
    bias = bias_ref[...]
    tot_r = lax.broadcasted_iota(jnp.int32, (2 * _LANES, _LANES), 0)
    tot_c = lax.broadcasted_iota(jnp.int32, (2 * _LANES, _LANES), 1)
    neg_total = jnp.where((tot_r & (hh - 1)) == (tot_c & (hh - 1)), -1.0, 0.0).astype(_BF16)
    sums_mat = jnp.concatenate([_neg_suffix_matrix(_LANES, classes=hh), neg_total], axis=1)
    row = lax.broadcasted_iota(jnp.int32, (hh, cols), 0)
    col = lax.broadcasted_iota(jnp.int32, (hh, cols), 1)
    real = (col & (hh - 1)) == row

    suffix = carry_ref[...]
    acc = acc_ref[...]
    for s in range(pages):
        kp = k_refs[s][...].astype(_BF16)
        vp = v_refs[s][...].astype(_BF16)
        z = lax.dot_general(q, kp, (((1,), (1,)), ((), ())), preferred_element_type=_F32) + bias
        sp = _neg_softplus2(z)
        stacked = jnp.concatenate([sp[:, i * _LANES:(i + 1) * _LANES] for i in range(nblk)], axis=0)
        hi, lo = _split_bf16(stacked)
        sums = jnp.dot(jnp.concatenate([hi, lo], axis=1), sums_mat, preferred_element_type=_F32)
        later = [None] * nblk
        for i in range(nblk - 1, -1, -1):
            later[i] = sums[i * hh:(i + 1) * hh, :_LANES] + suffix
            suffix = suffix + sums[i * hh:(i + 1) * hh, _LANES:]
        w = jnp.exp2(z + jnp.concatenate(later, axis=1))
        acc = acc + jnp.dot(jnp.where(real, w, 0.0).astype(_BF16), vp, preferred_element_type=_F32)
    carry_ref[...] = suffix
    acc_ref[...] = acc

    @pl.when(p == n_steps - 1)
    def _():
        o_ref[...] = acc


def _attn_sample(q, cache_k, cache_v, page_table, bias, layer, name="attn_sample"):
    b, hh, dh = q.shape
    assert hh & (hh - 1) == 0 and _LANES % hh == 0
    depth, n_phys, page = cache_k.shape[:3]
    n_pages = page_table.shape[1]
    ck = cache_k.reshape(depth, n_phys, page * hh, dh)
    cv = cache_v.reshape(depth, n_phys, page * hh, dh)
    pages = _SAMPLE_PAGES_PER_STEP if n_pages % _SAMPLE_PAGES_PER_STEP == 0 else 1
    n_steps = n_pages // pages

    def page_spec(s):
        return pl.BlockSpec((None, None, page * hh, dh),
                            lambda i, p, pt: (layer, pt[i * n_pages + (n_pages - 1 - (p * pages + s))], 0, 0))

    page_specs = [page_spec(s) for s in range(pages)]
    grid_spec = pltpu.PrefetchScalarGridSpec(
        num_scalar_prefetch=1, grid=(b, n_steps),
        in_specs=[pl.BlockSpec((None, hh, dh), lambda i, p, pt: (i, 0, 0)),
                  pl.BlockSpec((hh, 1), lambda i, p, pt: (0, 0))] + page_specs + page_specs,
        out_specs=pl.BlockSpec((None, hh, dh), lambda i, p, pt: (i, 0, 0)),
        scratch_shapes=[pltpu.VMEM((hh, dh), _F32), pltpu.VMEM((hh, _LANES), _F32)])
    return pl.pallas_call(
        functools.partial(_attn_sample_kernel, n_heads=hh, n_steps=n_steps, pages=pages),
        grid_spec=grid_spec, out_shape=jax.ShapeDtypeStruct((b, hh, dh), _F32),
        compiler_params=_params("parallel", "arbitrary"), name=name)(
            page_table.reshape(-1), q, (bias * _LOG2E).reshape(hh, 1), *([ck] * pages), *([cv] * pages))


_HALO = 32


def _conv_prompt_kernel(u_ref, prev_ref, hist_ref, w_ref, b_ref, g_ref, beta_ref, o_ref, win_ref, y_ref,
                        *, taps, tb, cw):
    i = pl.program_id(0)
    bt, ch = u_ref.shape
    lead = _HALO - (taps - 1)
    win_ref[0:_HALO, :] = jnp.where(i == 0, hist_ref[...], prev_ref[...])
    win_ref[_HALO:, :] = u_ref[...]
    by_residue = [[k for k in range(taps) if (k + lead) % _SUBLANES == res] for res in range(_SUBLANES)]
    for c0 in range(0, ch, cw):
        for t0 in range(0, bt, tb):
            acc = jnp.zeros((tb // _SUBLANES, _SUBLANES, cw), _F32)
            for res, ks in enumerate(by_residue):
                if not ks:
                    continue
                reach = ks[-1] + lead - res
                shifted = win_ref[t0 + res:t0 + res + tb + reach, c0:c0 + cw]
                shifted = shifted.reshape((tb + reach) // _SUBLANES, _SUBLANES, cw)
                for k in ks:
                    a0 = (k + lead - res) // _SUBLANES
                    acc = acc + shifted[a0:a0 + tb // _SUBLANES] * w_ref[k, :, c0:c0 + cw][None]
            y_ref[t0:t0 + tb, c0:c0 + cw] = acc.reshape(tb, cw)
    y = _layer_norm(y_ref[...] + b_ref[...], g_ref[...], beta_ref[...])
    o_ref[...] = (y * _sigmoid(y)).astype(o_ref.dtype)


def _conv_prompt(u, hist, conv_w, conv_b, ln_g, ln_b, out_dtype, name="conv_prompt"):
    t, ch = u.shape
    taps = conv_w.shape[0]
    assert taps - 1 <= _HALO and t % _HALO == 0
    bt = _blk(t, 128)
    hist_pad = jnp.concatenate([jnp.zeros((_HALO - (taps - 1), ch), u.dtype), hist.astype(u.dtype)], axis=0)
    ratio = bt // _HALO
    vec = pl.BlockSpec((1, ch), lambda i: (0, 0))
    kernel = functools.partial(_conv_prompt_kernel, taps=taps, tb=min(bt, 64), cw=min(ch, 256))
    return pl.pallas_call(
        kernel, grid=(t // bt,),
        in_specs=[pl.BlockSpec((bt, ch), lambda i: (i, 0)),
                  pl.BlockSpec((_HALO, ch), lambda i: (jnp.maximum(i * ratio - 1, 0), 0)),
                  pl.BlockSpec((_HALO, ch), lambda i: (0, 0)),
                  pl.BlockSpec((taps, _SUBLANES, ch), lambda i: (0, 0, 0)), vec, vec, vec],
        out_specs=pl.BlockSpec((bt, ch), lambda i: (i, 0)),
        out_shape=jax.ShapeDtypeStruct((t, ch), out_dtype),
        scratch_shapes=[pltpu.VMEM((bt + _HALO, ch), _F32), pltpu.VMEM((bt, ch), _F32)],
        compiler_params=_params("parallel"), name=name)(
            u, u, hist_pad, jnp.broadcast_to(conv_w[:, None, :], (taps, _SUBLANES, ch)),
            conv_b.reshape(1, ch), ln_g.reshape(1, ch), ln_b.reshape(1, ch))


def _conv_sample_kernel(u_ref, s_ref, w_ref, b_ref, g_ref, beta_ref, o_ref, ns_ref, *, taps):
    s = s_ref[...]
    u = u_ref[...]
    conv = jnp.sum(s * w_ref[0:taps - 1, :], axis=0, keepdims=True) + u * w_ref[taps - 1:taps, :]
    y = _layer_norm(conv + b_ref[...], g_ref[...], beta_ref[...])
    o_ref[...] = y * _sigmoid(y)
    ns_ref[0:taps - 2, :] = s_ref[1:taps - 1, :]
    ns_ref[taps - 2:taps - 1, :] = u


def _conv_sample(u, state, conv_w, conv_b, ln_g, ln_b, name="conv_sample"):
    b, ch = u.shape
    taps = conv_w.shape[0]
    vec = pl.BlockSpec((1, ch), lambda i: (0, 0))
    row = pl.BlockSpec((None, 1, ch), lambda i: (i, 0, 0))
    st = pl.BlockSpec((None, taps - 1, ch), lambda i: (i, 0, 0))
    out, new_state = pl.pallas_call(
        functools.partial(_conv_sample_kernel, taps=taps), grid=(b,),
        in_specs=[row, st, pl.BlockSpec((taps, ch), lambda i: (0, 0)), vec, vec, vec],
        out_specs=[row, st],
        out_shape=[jax.ShapeDtypeStruct((b, 1, ch), _F32), jax.ShapeDtypeStruct((b, taps - 1, ch), _F32)],
        compiler_params=_params("parallel"), name=name)(
            u.reshape(b, 1, ch), state, conv_w, conv_b.reshape(1, ch), ln_g.reshape(1, ch), ln_b.reshape(1, ch))
    return out.reshape(b, ch), new_state


_DMA_UNROLL = 8


def _row_copy(src_hbm, src_row, dst_ref, dst_row, sem):
    return pltpu.make_async_copy(src_hbm.at[pl.ds(src_row, 1), :], dst_ref.at[pl.ds(dst_row, 1), :], sem)


def _moe_up_kernel(be_ref, na_ref, first_ref, rank_ref, next_ref, nd_ref, tok_ref, x_hbm, w1_hbm, w3_hbm, o_ref,
                   xbuf_ref, xb_ref, w1buf_ref, w3buf_ref, sem, wsem, *, rows, kc, layer):
    j = pl.program_id(0)
    i = pl.program_id(1)
    n_active = na_ref[0]
    slot = i % 2
    d, tn = w1buf_ref.shape[1], w1buf_ref.shape[2]

    def start(blk, s):
        def body(r, _):
            _row_copy(x_hbm, tok_ref[blk * rows + r], xbuf_ref.at[s], r, sem.at[s]).start()
            return 0
        lax.fori_loop(0, rows, body, 0, unroll=_DMA_UNROLL)

    def weight_copies(e, col, s):
        cols = pl.ds(pl.multiple_of(col * tn, tn), tn)
        return (pltpu.make_async_copy(w1_hbm.at[layer, e, :, cols], w1buf_ref.at[s], wsem.at[s]),
                pltpu.make_async_copy(w3_hbm.at[layer, e, :, cols], w3buf_ref.at[s], wsem.at[s]))

    @pl.when((i == 0) & (n_active > 0))
    def _():
        start(0, 0)

    @pl.when(i + 1 < n_active)
    def _():
        start(i + 1, 1 - slot)

    @pl.when(i < n_active)
    def _():
        n_distinct = nd_ref[0]
        g = j * n_distinct + rank_ref[i]
        ws = g % 2

        @pl.when(first_ref[i] == 1)
        def _():
            @pl.when(g == 0)
            def _():
                for cp in weight_copies(be_ref[i], j, ws):
                    cp.start()

            @pl.when(g + 1 < pl.num_programs(0) * n_distinct)
            def _():
                wrap = (rank_ref[i] + 1 == n_distinct).astype(jnp.int32)
                for cp in weight_copies(next_ref[i], j + wrap, 1 - ws):
                    cp.start()

            for cp in weight_copies(be_ref[i], j, ws):
                cp.wait()

        def drain(r, _):
            _row_copy(x_hbm, 0, xbuf_ref.at[slot], r, sem.at[slot]).wait()
            return 0
        lax.fori_loop(0, rows, drain, 0, unroll=_DMA_UNROLL)
        xb_ref[...] = xbuf_ref[slot].astype(_BF16)
        a = jnp.zeros(o_ref.shape, _F32)
        b = jnp.zeros(o_ref.shape, _F32)
        for c0 in range(0, d, kc):
            x = xb_ref[:, c0:c0 + kc]
            a = a + _dot(x, w1buf_ref[ws, c0:c0 + kc, :].astype(_BF16))
            b = b + _dot(x, w3buf_ref[ws, c0:c0 + kc, :].astype(_BF16))
        o_ref[...] = (a * _sigmoid(a) * b).astype(o_ref.dtype)

    @pl.when(i >= n_active)
    def _():
        o_ref[...] = jnp.zeros_like(o_ref)


def _expert_changed(be_ref, i):
    prev = be_ref[jnp.maximum(i - 1, 0)]
    return (i == 0) | (be_ref[i] != prev)


def _gmm2_kernel(be_ref, na_ref, h_ref, w2_ref, o_ref, w2b_ref):
    i = pl.program_id(1)

    @pl.when(i < na_ref[0])
    def _():
        @pl.when(_expert_changed(be_ref, i))
        def _():
            w2b_ref[...] = w2_ref[...].astype(_BF16)

        o_ref[...] = _dot(h_ref[...], w2b_ref[...])

    @pl.when(i >= na_ref[0])
    def _():
        o_ref[...] = jnp.zeros_like(o_ref)


def _expert_ffn(x, row_tok, block_e, n_active, w1, w3, w2, layer):
    n_rows = row_tok.shape[0]
    d = x.shape[1]
    f = w1.shape[-1]
    bm = _MOE_ROWS
    tn1 = _blk(f, 512)
    tn2 = _blk(d, 2048)
    idx = jnp.arange(n_rows // bm, dtype=jnp.int32)
    first = ((idx < n_active[0]) & ((idx == 0) | (block_e != jnp.roll(block_e, 1)))).astype(jnp.int32)
    rank = jnp.maximum(jnp.cumsum(first) - 1, 0).astype(jnp.int32)
    n_distinct = jnp.sum(first).astype(jnp.int32)
    distinct_e = jnp.zeros_like(block_e).at[rank].set(block_e)
    next_e = distinct_e[(rank + 1) % jnp.maximum(n_distinct, 1)]
    any_spec = pl.BlockSpec(memory_space=pl.ANY)
    hidden = pl.pallas_call(
        functools.partial(_moe_up_kernel, rows=bm, kc=_blk(d, 512), layer=layer),
        grid_spec=pltpu.PrefetchScalarGridSpec(
            num_scalar_prefetch=7, grid=(f // tn1, n_rows // bm),
            in_specs=[any_spec, any_spec, any_spec],
            out_specs=pl.BlockSpec((bm, tn1), lambda j, i, *_: (i, j)),
            scratch_shapes=[pltpu.VMEM((2, bm, d), x.dtype), pltpu.VMEM((bm, d), _BF16),
                            pltpu.VMEM((2, d, tn1), w1.dtype), pltpu.VMEM((2, d, tn1), w3.dtype),
                            pltpu.SemaphoreType.DMA((2,)), pltpu.SemaphoreType.DMA((2,))]),
        out_shape=jax.ShapeDtypeStruct((n_rows, f), _BF16),
        compiler_params=_params("arbitrary", "arbitrary"), name="moe_up")(
            block_e, n_active, first, rank, next_e, n_distinct.reshape(1), row_tok, x, w1, w3)

    def last_active(i, na):
        return jnp.minimum(i, jnp.maximum(na[0] - 1, 0))

    return pl.pallas_call(
        _gmm2_kernel,
        grid_spec=pltpu.PrefetchScalarGridSpec(
            num_scalar_prefetch=2, grid=(d // tn2, n_rows // bm),
            in_specs=[pl.BlockSpec((bm, f), lambda j, i, be, na: (last_active(i, na), 0)),
                      pl.BlockSpec((None, None, f, tn2), lambda j, i, be, na: (layer, be[i], 0, j))],
            out_specs=pl.BlockSpec((bm, tn2), lambda j, i, be, na: (i, j)),
            scratch_shapes=[pltpu.VMEM((f, tn2), _BF16)]),
        out_shape=jax.ShapeDtypeStruct((n_rows, d), _F32),
        compiler_params=_params("arbitrary", "arbitrary"), name="moe_down")(block_e, n_active, hidden, w2)


def _combine_kernel(pos_ref, x_ref, r_ref, g_ref, b_ref, ys_hbm, o_ref, buf_ref, sem, *, bt, alpha):
    i = pl.program_id(0)
    slot = i % 2

    def start(blk, s):
        def body(r, _):
            for k in range(TOP_K):
                _row_copy(ys_hbm, pos_ref[(blk * bt + r) * TOP_K + k], buf_ref.at[s], k * bt + r, sem.at[s]).start()
            return 0
        lax.fori_loop(0, bt, body, 0, unroll=_DMA_UNROLL)

    @pl.when(i == 0)
    def _():
        start(0, 0)

    @pl.when(i + 1 < pl.num_programs(0))
    def _():
        start(i + 1, 1 - slot)

    def drain(r, _):
        _row_copy(ys_hbm, 0, buf_ref.at[slot], r, sem.at[slot]).wait()
        return 0

    lax.fori_loop(0, TOP_K * bt, drain, 0, unroll=_DMA_UNROLL)
    slab = r_ref[...]
    ffn = buf_ref[slot, 0:bt, :] * slab[:, 2:3]
    for k in range(1, TOP_K):
        ffn = ffn + buf_ref[slot, k * bt:(k + 1) * bt, :] * slab[:, 2 + k:3 + k]
    o_ref[...] = _layer_norm(alpha * x_ref[...] + ffn, g_ref[...], b_ref[...])


def _combine_ln(x1, row0, m, slab, pos, ys, g, b, alpha, name="moe_combine"):
    d = x1.shape[1]
    bt = _blk(m, 128)
    assert row0 % bt == 0
    vec = pl.BlockSpec((1, d), lambda i, pos: (0, 0))
    grid_spec = pltpu.PrefetchScalarGridSpec(
        num_scalar_prefetch=1, grid=(m // bt,),
        in_specs=[pl.BlockSpec((bt, d), lambda i, pos: (i + row0 // bt, 0)),
                  pl.BlockSpec((bt, _LANES), lambda i, pos: (i, 0)), vec, vec,
                  pl.BlockSpec(memory_space=pl.ANY)],
        out_specs=pl.BlockSpec((bt, d), lambda i, pos: (i, 0)),
        scratch_shapes=[pltpu.VMEM((2, TOP_K * bt, d), _F32), pltpu.SemaphoreType.DMA((2,))])
    return pl.pallas_call(
        functools.partial(_combine_kernel, bt=bt, alpha=alpha), grid_spec=grid_spec,
        out_shape=jax.ShapeDtypeStruct((m, d), _F32),
        compiler_params=_params("arbitrary"), name=name)(pos, x1, slab, g.reshape(1, d), b.reshape(1, d), ys)


def _dispatch_plan(expert, n_experts, bm):
    a = expert.shape[0]
    n_blocks = -(-(a + n_experts * (bm - 1)) // bm)
    n_rows = n_blocks * bm
    onehot = (expert[:, None] == jnp.arange(n_experts, dtype=jnp.int32)[None, :]).astype(jnp.int32)
    rank = jnp.take_along_axis(jnp.cumsum(onehot, axis=0) - onehot, expert[:, None], axis=1)[:, 0]
    counts = jnp.sum(onehot, axis=0)
    padded = (counts + bm - 1) // bm * bm
    pad_end = jnp.cumsum(padded)
    pad_start = pad_end - padded
    dest = (pad_start[expert] + rank).astype(jnp.int32)
    row_src = jnp.full((n_rows,), -1, jnp.int32).at[dest].set(jnp.arange(a, dtype=jnp.int32))
    n_active = (pad_end[-1] // bm).astype(jnp.int32)
    blk = jnp.minimum(jnp.arange(n_blocks, dtype=jnp.int32), jnp.maximum(n_active - 1, 0))
    block_e = jnp.minimum(jnp.searchsorted(pad_end, blk * bm, side="right"), n_experts - 1).astype(jnp.int32)
    return dest, row_src, block_e, n_active.reshape(1)


def _branches(x, w_in, w_att_out, w_conv_out, w_out, b_gate, attn_fn, conv_fn, widths, kv_bf16):
    a, c, d = widths
    scale = float(a // attn_fn.n_heads) ** -0.5 * _LOG2E
    act = x.dtype
    q = _project(x, w_in, 0, a, act, scale=scale, name="proj_q")
    if kv_bf16:
        k, k_b = _project_kv(x, w_in, a, a, "proj_k")
        v, v_b = _project_kv(x, w_in, 2 * a, a, "proj_v")
    else:
        k = k_b = _project(x, w_in, a, a, _F32, name="proj_k")
        v = v_b = _project(x, w_in, 2 * a, a, _F32, name="proj_v")
    u = _project_glu(x, w_in, 3 * a, c)
    gates = _project_gate(x, w_in, 3 * a + 2 * c, 2 * d, b_gate)
    att = attn_fn(q, k_b, v_b)
    cnv, conv_state = conv_fn(u)
    pre = _mix(att.astype(act), cnv.astype(act), w_att_out, w_conv_out, gates, act)
    mixed = _project(pre, w_out, 0, d, _F32, name="proj_out")
    return mixed, k, v, conv_state


class _PromptAttn:
    def __init__(self, bias, n_heads):
        self.bias, self.n_heads = bias, n_heads

    def __call__(self, q, k, v):
        return _attn_prompt(q, k, v, self.bias, self.n_heads)


class _SampleAttn:
    def __init__(self, bias, n_heads, cache_k, cache_v, page_table, layer):
        self.bias, self.n_heads = bias, n_heads
        self.cache_k, self.cache_v, self.page_table, self.layer = cache_k, cache_v, page_table, layer

    def __call__(self, q, k, v):
        b, a = q.shape
        out = _attn_sample(q.reshape(b, self.n_heads, a // self.n_heads), self.cache_k, self.cache_v,
                           self.page_table, self.bias, self.layer)
        return out.reshape(b, a)


def kernel(x_prompt, x_sample, cache_k, cache_v, state_conv, page_table, w_in, b_sb, b_gate, conv_w, conv_b,
           conv_ln_g, conv_ln_b, w_att_out, w_conv_out, w_out, ln1_g, ln1_b, w_router_group, b_router_group,
           w_router_expert, b_router_expert, w1, w3, w2, ln2_g, ln2_b):
    depth, d, _ = w_in.shape
    n_heads, d_head = cache_k.shape[3], cache_k.shape[4]
    a = n_heads * d_head
    c = conv_w.shape[2]
    taps = conv_w.shape[1]
    n_groups = w_router_group.shape[2]
    n_experts = w_router_expert.shape[2]
    per_group = n_experts // n_groups
    assert n_groups + n_experts <= _LANES
    alpha = (2.0 * depth) ** 0.25
    bsz, seq, _ = x_prompt.shape
    dec, dec_seq, _ = x_sample.shape
    assert bsz == 1 and dec_seq == 1
    n_p, n_s = bsz * seq, dec * dec_seq

    y_p = x_prompt.reshape(n_p, d)
    y_s = x_sample.reshape(n_s, d)
    outs = {k: [] for k in ("kp", "vp", "cp", "ks", "vs", "cs")}
    for l in range(depth):
        win, wao, wco, wo = w_in[l], w_att_out[l], w_conv_out[l], w_out[l]
        hist0 = jnp.zeros((taps - 1, c), _F32)
        conv_p = lambda u, l=l, hist0=hist0: (
            _conv_prompt(u, hist0, conv_w[l], conv_b[l], conv_ln_g[l], conv_ln_b[l], _BF16), u[seq - (taps - 1):])
        weights = (win.astype(_BF16), wao.astype(_BF16), wco.astype(_BF16), wo.astype(_BF16))
        mixed_p, k_p, v_p, cs_p = _branches(
            y_p.astype(_BF16), *weights, b_gate[l], _PromptAttn(b_sb[l], n_heads), conv_p, (a, c, d), True)
        conv_s = lambda u, l=l: _conv_sample(u, state_conv[l], conv_w[l], conv_b[l], conv_ln_g[l], conv_ln_b[l])
        mixed_s, k_s, v_s, cs_s = _branches(
            y_s.astype(_BF16), *weights, b_gate[l],
            _SampleAttn(b_sb[l], n_heads, cache_k, cache_v, page_table, l), conv_s, (a, c, d), False)

        n_all = n_p + -(-n_s // _LN_ROWS) * _LN_ROWS
        x1 = _res_ln(y_p, mixed_p, ln1_g[l], ln1_b[l], alpha, n_all, name="ln1_prompt")
        x1 = _res_ln(y_s, mixed_s, ln1_g[l], ln1_b[l], alpha, n_all, row0=n_p, into=x1, name="ln1_sample")

        w_r = jnp.zeros((d, _LANES), _F32).at[:, :n_groups].set(w_router_group[l]) \
                 .at[:, n_groups:n_groups + n_experts].set(w_router_expert[l])
        b_r = jnp.zeros((1, _LANES), _F32).at[0, :n_groups].set(b_router_group[l]) \
                 .at[0, n_groups:n_groups + n_experts].set(b_router_expert[l])
        slab_p = _router(x1, 0, n_p, w_r, b_r, n_groups, per_group, name="router_prompt")
        slab_s = _router(x1, n_p, n_s, w_r, b_r, n_groups, per_group, name="router_sample")
        expert = jnp.concatenate([slab_p[:, :TOP_K], slab_s[:, :TOP_K]], axis=0).astype(jnp.int32).reshape(-1)
        dest, row_src, block_e, n_active = _dispatch_plan(expert, n_experts, _MOE_ROWS)
        row_tok = jnp.maximum(row_src, 0) // TOP_K
        ys = _expert_ffn(x1, row_tok, block_e, n_active, w1, w3, w2, l)
        y_p = _combine_ln(x1, 0, n_p, slab_p, dest[:n_p * TOP_K], ys, ln2_g[l], ln2_b[l], alpha,
                          name="combine_prompt")
        y_s = _combine_ln(x1, n_p, n_s, slab_s, dest[n_p * TOP_K:], ys, ln2_g[l], ln2_b[l], alpha,
                          name="combine_sample")

        outs["kp"].append(k_p.reshape(bsz, seq, n_heads, d_head))
        outs["vp"].append(v_p.reshape(bsz, seq, n_heads, d_head))
        outs["cp"].append(cs_p.reshape(bsz, taps - 1, c))
        outs["ks"].append(k_s.reshape(dec, dec_seq, n_heads, d_head))
        outs["vs"].append(v_s.reshape(dec, dec_seq, n_heads, d_head))
        outs["cs"].append(cs_s)
    return (y_p.reshape(bsz, seq, d), y_s.reshape(dec, dec_seq, d),
            jnp.stack(outs["kp"]), jnp.stack(outs["vp"]), jnp.stack(outs["cp"]),
            jnp.stack(outs["ks"]), jnp.stack(outs["vs"]), jnp.stack(outs["cs"]))
```

```python
import functools

import jax
import jax.numpy as jnp
from jax import lax
from jax.experimental import pallas as pl
from jax.experimental.pallas import tpu as pltpu

_F32 = jnp.float32
_BF16 = jnp.bfloat16

LN_EPS = 1e-5
TOP_K = 2
_VMEM_LIMIT_V7X = 56 * 1024 * 1024
_LANES = 128
_SUBLANES = 8
_MOE_ROWS = 256


def _params(*sem):
    return pltpu.CompilerParams(dimension_semantics=sem, vmem_limit_bytes=_VMEM_LIMIT_V7X)


def _blk(dim, pref):
    if dim <= pref:
        return dim
    assert dim % pref == 0, (dim, pref)
    return pref


def _split_bf16(x):
    hi = x.astype(_BF16)
    lo = (x - hi.astype(_F32)).astype(_BF16)
    return hi, lo


def _dot(x, w):
    assert x.dtype == w.dtype, (x.dtype, w.dtype)
    d = functools.partial(jnp.dot, preferred_element_type=_F32)
    if x.dtype == _BF16:
        return d(x, w)
    xh, xl = _split_bf16(x)
    wh, wl = _split_bf16(w)
    return d(xh, wh) + (d(xl, wh) + d(xh, wl))


def _sigmoid(x):
    return 1.0 / (1.0 + jnp.exp(-x))


def _layer_norm(v, g, b):
    mu = jnp.mean(v, axis=-1, keepdims=True)
    d = v - mu
    var = jnp.mean(d * d, axis=-1, keepdims=True)
    return d * lax.rsqrt(var + LN_EPS) * g + b


def _mm_kernel(x_ref, w_ref, o_ref, *, scale):
    acc = _dot(x_ref[...], w_ref[...])
    if scale != 1.0:
        acc = acc * scale
    o_ref[...] = acc.astype(o_ref.dtype)


def _mm2_kernel(x_ref, w_ref, o_ref, ob_ref):
    acc = _dot(x_ref[...], w_ref[...])
    o_ref[...] = acc
    ob_ref[...] = acc.astype(ob_ref.dtype)


def _glu_kernel(x_ref, wv_ref, wg_ref, o_ref):
    x = x_ref[...]
    o_ref[...] = _dot(x, wv_ref[...]) * _sigmoid(_dot(x, wg_ref[...]))


def _gate_kernel(x_ref, w_ref, b_ref, o_ref):
    o_ref[...] = _sigmoid(_dot(x_ref[...], w_ref[...]) + b_ref[...])


def _mix_kernel(a_ref, c_ref, wa_ref, wc_ref, ga_ref, gc_ref, o_ref):
    att = _dot(a_ref[...], wa_ref[...])
    cnv = _dot(c_ref[...], wc_ref[...])
    o_ref[...] = (ga_ref[...] * att + gc_ref[...] * cnv).astype(o_ref.dtype)


def _row_spec(bm, k):
    return pl.BlockSpec((bm, k), lambda j, i: (i, 0))


def _col_spec(k, bn, off_blocks=0):
    return pl.BlockSpec((k, bn), lambda j, i: (0, j + off_blocks))


def _tile_spec(bm, bn, off_blocks=0):
    return pl.BlockSpec((bm, bn), lambda j, i: (i, j + off_blocks))


def _dense_call(kernel, name, m, n, bm, bn, in_specs, out_dtypes, args):
    outs = [jax.ShapeDtypeStruct((m, n), dt) for dt in out_dtypes]
    specs = [_tile_spec(bm, bn) for _ in out_dtypes]
    single = len(outs) == 1
    return pl.pallas_call(
        kernel, grid=(n // bn, m // bm), in_specs=in_specs,
        out_specs=specs[0] if single else specs,
        out_shape=outs[0] if single else outs,
        compiler_params=_params("parallel", "parallel"), name=name)(*args)


def _dense_blocks(x, n):
    m, k = x.shape
    prec3 = x.dtype == _F32
    bm = _blk(m, 512)
    bn = _blk(n, 256 if prec3 else 1024)
    return m, k, bm, bn


def _project(x, w, col0, n, out_dtype, scale=1.0, name="proj"):
    m, k, bm, bn = _dense_blocks(x, n)
    return _dense_call(functools.partial(_mm_kernel, scale=scale), name, m, n, bm, bn,
                       [_row_spec(bm, k), _col_spec(k, bn, col0 // bn)], [out_dtype], (x, w))


def _project_kv(x, w, col0, n, name):
    m, k, bm, bn = _dense_blocks(x, n)
    return _dense_call(_mm2_kernel, name, m, n, bm, bn,
                       [_row_spec(bm, k), _col_spec(k, bn, col0 // bn)], [_F32, _BF16], (x, w))


def _project_glu(x, w, col0, c, name="glu"):
    m, k, bm, bn = _dense_blocks(x, c)
    return _dense_call(_glu_kernel, name, m, c, bm, bn,
                       [_row_spec(bm, k), _col_spec(k, bn, col0 // bn), _col_spec(k, bn, (col0 + c) // bn)],
                       [_F32], (x, w, w))


def _project_gate(x, w, col0, n, b_gate, name="gate"):
    m, k, bm, bn = _dense_blocks(x, n)
    b_spec = pl.BlockSpec((1, bn), lambda j, i: (0, j))
    return _dense_call(_gate_kernel, name, m, n, bm, bn,
                       [_row_spec(bm, k), _col_spec(k, bn, col0 // bn), b_spec], [_F32],
                       (x, w, b_gate.reshape(1, n)))


def _mix(att, cnv, w_att, w_cnv, gates, out_dtype, name="mix"):
    n = w_att.shape[1]
    m, ka, bm, bn = _dense_blocks(att, n)
    kc = cnv.shape[1]
    return _dense_call(_mix_kernel, name, m, n, bm, bn,
                       [_row_spec(bm, ka), _row_spec(bm, kc), _col_spec(ka, bn), _col_spec(kc, bn),
                        _tile_spec(bm, bn), _tile_spec(bm, bn, n // bn)],
                       [out_dtype], (att, cnv, w_att, w_cnv, gates, gates))


_LN_ROWS = 256


def _res_ln_kernel(x_ref, r_ref, g_ref, b_ref, *rest, alpha, n_blocks):
    o_ref = rest[-1]
    i = pl.program_id(0)

    @pl.when(i < n_blocks)
    def _():
        o_ref[...] = _layer_norm(alpha * x_ref[...] + r_ref[...], g_ref[...], b_ref[...])

    @pl.when(i >= n_blocks)
    def _():
        o_ref[...] = jnp.zeros_like(o_ref)


def _res_ln(x, r, g, b, alpha, total_rows, row0=0, into=None, name="res_ln"):
    m, d = x.shape
    bt = _blk(m, _LN_ROWS)
    assert row0 % bt == 0 and total_rows % bt == 0
    n_blocks = m // bt
    row = pl.BlockSpec((bt, d), lambda i: (jnp.minimum(i, n_blocks - 1), 0))
    vec = pl.BlockSpec((1, d), lambda i: (0, 0))
    args = [x, r, g.reshape(1, d), b.reshape(1, d)]
    in_specs = [row, row, vec, vec]
    aliases = {}
    grid = total_rows // bt
    if into is not None:
        assert into.shape == (total_rows, d)
        args.append(into)
        in_specs.append(pl.BlockSpec(memory_space=pl.ANY))
        aliases = {4: 0}
        grid = n_blocks
    else:
        assert row0 == 0
    return pl.pallas_call(
        functools.partial(_res_ln_kernel, alpha=alpha, n_blocks=n_blocks), grid=(grid,),
        in_specs=in_specs, out_specs=pl.BlockSpec((bt, d), lambda i: (i + row0 // bt, 0)),
        out_shape=jax.ShapeDtypeStruct((total_rows, d), _F32), input_output_aliases=aliases,
        compiler_params=_params("parallel"), name=name)(*args)


def _router_kernel(x_ref, w_ref, b_ref, o_ref, *, n_groups, per_group):
    logits = _dot(x_ref[...], w_ref[...]) + b_ref[...]
    lane = lax.broadcasted_iota(jnp.int32, logits.shape, 1).astype(_F32)
    neg = -jnp.inf
    big = 1e9

    def arg_max(v):
        m = jnp.max(v, axis=-1, keepdims=True)
        return m, jnp.min(jnp.where(v == m, lane, big), axis=-1, keepdims=True)

    glog = jnp.where(lane < n_groups, logits, neg)
    gmax, g_idx = arg_max(glog)
    g_p = 1.0 / jnp.sum(jnp.exp(glog - gmax), axis=-1, keepdims=True)
    lo = n_groups + g_idx * per_group
    elog = jnp.where((lane >= lo) & (lane < lo + per_group), logits, neg)
    m1, i1 = arg_max(elog)
    m2, i2 = arg_max(jnp.where(lane == i1, neg, elog))
    e2 = jnp.exp(m2 - m1)
    p1 = 1.0 / (1.0 + e2)
    p2 = e2 / (1.0 + e2)
    o_ref[...] = jnp.where(lane == 0, i1 - n_groups,
                 jnp.where(lane == 1, i2 - n_groups,
                 jnp.where(lane == 2, g_p * p1,
                 jnp.where(lane == 3, g_p * p2, 0.0))))


def _router(x1, row0, m, w_r, b_r, n_groups, per_group, name="router"):
    d = x1.shape[1]
    bt = _blk(m, 256)
    assert row0 % bt == 0
    return pl.pallas_call(
        functools.partial(_router_kernel, n_groups=n_groups, per_group=per_group), grid=(m // bt,),
        in_specs=[pl.BlockSpec((bt, d), lambda i: (i + row0 // bt, 0)),
                  pl.BlockSpec((d, _LANES), lambda i: (0, 0)),
                  pl.BlockSpec((1, _LANES), lambda i: (0, 0))],
        out_specs=pl.BlockSpec((bt, _LANES), lambda i: (i, 0)),
        out_shape=jax.ShapeDtypeStruct((m, _LANES), _F32),
        compiler_params=_params("parallel"), name=name)(x1, w_r, b_r)


_LOG2E = 1.4426950408889634
_SAMPLE_PAGES_PER_STEP = 8


def _neg_abs(z):
    return lax.bitcast_convert_type(lax.bitcast_convert_type(z, jnp.uint32) | jnp.uint32(0x80000000), _F32)


def _neg_softplus2(z):
    return jnp.maximum(z, 0.0) + jnp.log2(1.0 + jnp.exp2(_neg_abs(z)))


_MASKED_LOGIT = -1e30


def _neg_suffix_matrix(n, classes=1, copies=2):
    r = lax.broadcasted_iota(jnp.int32, (copies * n, n), 0) & (n - 1)
    c = lax.broadcasted_iota(jnp.int32, (copies * n, n), 1)
    keep = r >= c
    if classes > 1:
        keep = keep & ((r & (classes - 1)) == (c & (classes - 1)))
    return jnp.where(keep, -1.0, 0.0).astype(_BF16)


def _attn_prompt_kernel(bias_ref, q_ref, k_ref, v_ref, o_ref, acc_ref, carry_ref, z0_ref, z1_ref, s0_ref, s1_ref,
                        *, bq, sub):
    h = pl.program_id(0)
    qi = pl.program_id(1)
    bias = bias_ref[h]
    nsub = bq // sub
    neg_suffix = _neg_suffix_matrix(sub, copies=1)
    acc_ref[...] = jnp.zeros_like(acc_ref)
    carry_ref[...] = jnp.zeros_like(carry_ref)

    def front(j, z_ref, s_ref, masked=False):
        k0 = pl.multiple_of(j * bq, bq)
        z = lax.dot_general(q_ref[...], k_ref[pl.ds(k0, bq), :], (((1,), (1,)), ((), ())),
                            preferred_element_type=_F32) + bias
        if masked:
            q_pos = qi * bq + lax.broadcasted_iota(jnp.int32, (bq, bq), 0)
            k_pos = k0 + lax.broadcasted_iota(jnp.int32, (bq, bq), 1)
            z = jnp.where(k_pos < q_pos, z, _MASKED_LOGIT)
        z_ref[...] = z
        for s in range(nsub):
            s_ref[s * bq:(s + 1) * bq, :] = _neg_softplus2(z[:, s * sub:(s + 1) * sub]).astype(_BF16)

    def back(j, z_ref, s_ref):
        vb = v_ref[pl.ds(pl.multiple_of(j * bq, bq), bq), :]
        incl = jnp.dot(s_ref[...], neg_suffix, preferred_element_type=_F32)
        carry = carry_ref[...]
        ws = [None] * nsub
        for s in range(nsub - 1, -1, -1):
            inc = incl[s * bq:(s + 1) * bq, :]
            ws[s] = jnp.exp2(z_ref[:, s * sub:(s + 1) * sub] + inc + carry).astype(_BF16)
            carry = carry + inc[:, 0:1]
        carry_ref[...] = carry
        acc_ref[...] += jnp.dot(jnp.concatenate(ws, axis=1), vb, preferred_element_type=_F32)

    front(qi, z0_ref, s0_ref, masked=True)

    def pair(t, _):
        c = qi - 2 * t
        front(c - 1, z1_ref, s1_ref)
        back(c, z0_ref, s0_ref)
        front(c - 2, z0_ref, s0_ref)
        back(c - 1, z1_ref, s1_ref)
        return 0

    lax.fori_loop(0, qi // 2, pair, 0)

    @pl.when(qi % 2 == 1)
    def _():
        front(0, z1_ref, s1_ref)
        back(1, z0_ref, s0_ref)
        back(0, z1_ref, s1_ref)

    @pl.when(qi % 2 == 0)
    def _():
        back(0, z0_ref, s0_ref)

    o_ref[...] = acc_ref[...].astype(o_ref.dtype)


def _attn_prompt(q, k, v, bias, n_heads, name="attn_prompt"):
    t, a = q.shape
    dh = a // n_heads
    bq = _blk(t, 512)
    sub = _blk(bq, 256)
    assert sub & (sub - 1) == 0
    bias = bias * _LOG2E
    kernel = functools.partial(_attn_prompt_kernel, bq=bq, sub=sub)
    return pl.pallas_call(
        kernel, grid=(n_heads, t // bq),
        in_specs=[pl.BlockSpec(memory_space=pltpu.SMEM),
                  pl.BlockSpec((bq, dh), lambda h, i: (i, h)),
                  pl.BlockSpec((t, dh), lambda h, i: (0, h)),
                  pl.BlockSpec((t, dh), lambda h, i: (0, h))],
        out_specs=pl.BlockSpec((bq, dh), lambda h, i: (i, h)),
        out_shape=jax.ShapeDtypeStruct((t, a), _BF16),
        scratch_shapes=[pltpu.VMEM((bq, dh), _F32), pltpu.VMEM((bq, 1), _F32),
                        pltpu.VMEM((bq, bq), _F32), pltpu.VMEM((bq, bq), _F32),
                        pltpu.VMEM((bq * bq // sub, sub), _BF16), pltpu.VMEM((bq * bq // sub, sub), _BF16)],
        compiler_params=_params("parallel", "parallel"), name=name)(bias, q, k, v)


def _attn_sample_kernel(pt_ref, q_ref, bias_ref, *refs, n_heads, n_steps, pages):
    k_refs, v_refs = refs[:pages], refs[pages:2 * pages]
    o_ref, acc_ref, carry_ref = refs[2 * pages:]
    p = pl.program_id(1)
    hh = n_heads
    cols = k_refs[0].shape[0]
    nblk = cols // _LANES

    @pl.when(p == 0)
    def _():
        acc_ref[...] = jnp.zeros_like(acc_ref)
        carry_ref[...] = jnp.zeros_like(carry_ref)

    q = q_ref[...].astype(_BF16)
    bias = bias_ref[...]
    tot_r = lax.broadcasted_iota(jnp.int32, (2 * _LANES, _LANES), 0)
    tot_c = lax.broadcasted_iota(jnp.int32, (2 * _LANES, _LANES), 1)
    neg_total = jnp.where((tot_r & (hh - 1)) == (tot_c & (hh - 1)), -1.0, 0.0).astype(_BF16)
    sums_mat = jnp.concatenate([_neg_suffix_matrix(_LANES, classes=hh), neg_total], axis=1)
    row = lax.broadcasted_iota(jnp.int32, (hh, cols), 0)
    col = lax.broadcasted_iota(jnp.int32, (hh, cols), 1)
    real = (col & (hh - 1)) == row

    suffix = carry_ref[...]
    acc = acc_ref[...]
    for s in range(pages):
        kp = k_refs[s][...].astype(_BF16)
        vp = v_refs[s][...].astype(_BF16)
        z = lax.dot_general(q, kp, (((1,), (1,)), ((), ())), preferred_element_type=_F32) + bias
        sp = _neg_softplus2(z)
        stacked = jnp.concatenate([sp[:, i * _LANES:(i + 1) * _LANES] for i in range(nblk)], axis=0)
        hi, lo = _split_bf16(stacked)
        sums = jnp.dot(jnp.concatenate([hi, lo], axis=1), sums_mat, preferred_element_type=_F32)
        later = [None] * nblk
        for i in range(nblk - 1, -1, -1):
            later[i] = sums[i * hh:(i + 1) * hh, :_LANES] + suffix
            suffix = suffix + sums[i * hh:(i + 1) * hh, _LANES:]
        w = jnp.exp2(z + jnp.concatenate(later, axis=1))
        acc = acc + jnp.dot(jnp.where(real, w, 0.0).astype(_BF16), vp, preferred_element_type=_F32)
    carry_ref[...] = suffix
    acc_ref[...] = acc

    @pl.when(p == n_steps - 1)
    def _():
        o_ref[...] = acc


def _attn_sample(q, cache_k, cache_v, page_table, bias, layer, name="attn_sample"):
    b, hh, dh = q.shape
    assert hh & (hh - 1) == 0 and _LANES % hh == 0
    depth, n_phys, page = cache_k.shape[:3]
    n_pages = page_table.shape[1]
    ck = cache_k.reshape(depth, n_phys, page * hh, dh)
    cv = cache_v.reshape(depth, n_phys, page * hh, dh)
    pages = _SAMPLE_PAGES_PER_STEP if n_pages % _SAMPLE_PAGES_PER_STEP == 0 else 1
    n_steps = n_pages // pages

    def page_spec(s):
        return pl.BlockSpec((None, None, page * hh, dh),
                            lambda i, p, pt: (layer, pt[i * n_pages + (n_pages - 1 - (p * pages + s))], 0, 0))

    page_specs = [page_spec(s) for s in range(pages)]
    grid_spec = pltpu.PrefetchScalarGridSpec(
        num_scalar_prefetch=1, grid=(b, n_steps),
        in_specs=[pl.BlockSpec((None, hh, dh), lambda i, p, pt: (i, 0, 0)),
                  pl.BlockSpec((hh, 1), lambda i, p, pt: (0, 0))] + page_specs + page_specs,
        out_specs=pl.BlockSpec((None, hh, dh), lambda i, p, pt: (i, 0, 0)),
        scratch_shapes=[pltpu.VMEM((hh, dh), _F32), pltpu.VMEM((hh, _LANES), _F32)])
    return pl.pallas_call(
        functools.partial(_attn_sample_kernel, n_heads=hh, n_steps=n_steps, pages=pages),
        grid_spec=grid_spec, out_shape=jax.ShapeDtypeStruct((b, hh, dh), _F32),
        compiler_params=_params("parallel", "arbitrary"), name=name)(
            page_table.reshape(-1), q, (bias * _LOG2E).reshape(hh, 1), *([ck] * pages), *([cv] * pages))


_HALO = 32


def _conv_prompt_kernel(u_ref, prev_ref, hist_ref, w_ref, b_ref, g_ref, beta_ref, o_ref, win_ref, y_ref,
                        *, taps, tb, cw):
    i = pl.program_id(0)
    bt, ch = u_ref.shape
    lead = _HALO - (taps - 1)
    win_ref[0:_HALO, :] = jnp.where(i == 0, hist_ref[...], prev_ref[...])
    win_ref[_HALO:, :] = u_ref[...]
    by_residue = [[k for k in range(taps) if (k + lead) % _SUBLANES == res] for res in range(_SUBLANES)]
    for c0 in range(0, ch, cw):
        for t0 in range(0, bt, tb):
            acc = jnp.zeros((tb // _SUBLANES, _SUBLANES, cw), _F32)
            for res, ks in enumerate(by_residue):
                if not ks:
                    continue
                reach = ks[-1] + lead - res
                shifted = win_ref[t0 + res:t0 + res + tb + reach, c0:c0 + cw]
                shifted = shifted.reshape((tb + reach) // _SUBLANES, _SUBLANES, cw)
                for k in ks:
                    a0 = (k + lead - res) // _SUBLANES
                    acc = acc + shifted[a0:a0 + tb // _SUBLANES] * w_ref[k, :, c0:c0 + cw][None]
            y_ref[t0:t0 + tb, c0:c0 + cw] = acc.reshape(tb, cw)
    y = _layer_norm(y_ref[...] + b_ref[...], g_ref[...], beta_ref[...])
    o_ref[...] = (y * _sigmoid(y)).astype(o_ref.dtype)


def _conv_prompt(u, hist, conv_w, conv_b, ln_g, ln_b, out_dtype, name="conv_prompt"):
    t, ch = u.shape
    taps = conv_w.shape[0]
    assert taps - 1 <= _HALO and t % _HALO == 0
    bt = _blk(t, 128)
    hist_pad = jnp.concatenate([jnp.zeros((_HALO - (taps - 1), ch), u.dtype), hist.astype(u.dtype)], axis=0)
    ratio = bt // _HALO
    vec = pl.BlockSpec((1, ch), lambda i: (0, 0))
    kernel = functools.partial(_conv_prompt_kernel, taps=taps, tb=min(bt, 64), cw=min(ch, 256))
    return pl.pallas_call(
        kernel, grid=(t // bt,),
        in_specs=[pl.BlockSpec((bt, ch), lambda i: (i, 0)),
                  pl.BlockSpec((_HALO, ch), lambda i: (jnp.maximum(i * ratio - 1, 0), 0)),
                  pl.BlockSpec((_HALO, ch), lambda i: (0, 0)),
                  pl.BlockSpec((taps, _SUBLANES, ch), lambda i: (0, 0, 0)), vec, vec, vec],
        out_specs=pl.BlockSpec((bt, ch), lambda i: (i, 0)),
        out_shape=jax.ShapeDtypeStruct((t, ch), out_dtype),
        scratch_shapes=[pltpu.VMEM((bt + _HALO, ch), _F32), pltpu.VMEM((bt, ch), _F32)],
        compiler_params=_params("parallel"), name=name)(
            u, u, hist_pad, jnp.broadcast_to(conv_w[:, None, :], (taps, _SUBLANES, ch)),
            conv_b.reshape(1, ch), ln_g.reshape(1, ch), ln_b.reshape(1, ch))


def _conv_sample_kernel(u_ref, s_ref, w_ref, b_ref, g_ref, beta_ref, o_ref, ns_ref, *, taps):
    s = s_ref[...]
    u = u_ref[...]
    conv = jnp.sum(s * w_ref[0:taps - 1, :], axis=0, keepdims=True) + u * w_ref[taps - 1:taps, :]
    y = _layer_norm(conv + b_ref[...], g_ref[...], beta_ref[...])
    o_ref[...] = y * _sigmoid(y)
    ns_ref[0:taps - 2, :] = s_ref[1:taps - 1, :]
    ns_ref[taps - 2:taps - 1, :] = u


def _conv_sample(u, state, conv_w, conv_b, ln_g, ln_b, name="conv_sample"):
    b, ch = u.shape
    taps = conv_w.shape[0]
    vec = pl.BlockSpec((1, ch), lambda i: (0, 0))
    row = pl.BlockSpec((None, 1, ch), lambda i: (i, 0, 0))
    st = pl.BlockSpec((None, taps - 1, ch), lambda i: (i, 0, 0))
    out, new_state = pl.pallas_call(
        functools.partial(_conv_sample_kernel, taps=taps), grid=(b,),
        in_specs=[row, st, pl.BlockSpec((taps, ch), lambda i: (0, 0)), vec, vec, vec],
        out_specs=[row, st],
        out_shape=[jax.ShapeDtypeStruct((b, 1, ch), _F32), jax.ShapeDtypeStruct((b, taps - 1, ch), _F32)],
        compiler_params=_params("parallel"), name=name)(
            u.reshape(b, 1, ch), state, conv_w, conv_b.reshape(1, ch), ln_g.reshape(1, ch), ln_b.reshape(1, ch))
    return out.reshape(b, ch), new_state


_DMA_UNROLL = 8


def _row_copy(src_hbm, src_row, dst_ref, dst_row, sem):
    return pltpu.make_async_copy(src_hbm.at[pl.ds(src_row, 1), :], dst_ref.at[pl.ds(dst_row, 1), :], sem)


def _moe_up_kernel(be_ref, na_ref, first_ref, rank_ref, next_ref, nd_ref, nv_ref, tok_ref, x_hbm, w1_hbm, w3_hbm,
                   o_ref, xbuf_ref, xb_ref, w1buf_ref, w3buf_ref, sem, wsem, *, rows, kc, layer):
    j = pl.program_id(0)
    i = pl.program_id(1)
    n_active = na_ref[0]
    slot = i % 2
    d, tn = w1buf_ref.shape[1], w1buf_ref.shape[2]

    def start(blk, s):
        def body(r, _):
            _row_copy(x_hbm, tok_ref[blk * rows + r], xbuf_ref.at[s], r, sem.at[s]).start()
            return 0
        lax.fori_loop(0, nv_ref[blk], body, 0)

    @pl.when((j == 0) & (i == 0))
    def _():
        xbuf_ref[...] = jnp.zeros_like(xbuf_ref)

    def weight_copies(e, col, s):
        cols = pl.ds(pl.multiple_of(col * tn, tn), tn)
        return (pltpu.make_async_copy(w1_hbm.at[layer, e, :, cols], w1buf_ref.at[s], wsem.at[s]),
                pltpu.make_async_copy(w3_hbm.at[layer, e, :, cols], w3buf_ref.at[s], wsem.at[s]))

    @pl.when((i == 0) & (n_active > 0))
    def _():
        start(0, 0)

    @pl.when(i + 1 < n_active)
    def _():
        start(i + 1, 1 - slot)

    @pl.when(i < n_active)
    def _():
        n_distinct = nd_ref[0]
        g = j * n_distinct + rank_ref[i]
        ws = g % 2

        @pl.when(first_ref[i] == 1)
        def _():
            @pl.when(g == 0)
            def _():
                for cp in weight_copies(be_ref[i], j, ws):
                    cp.start()

            @pl.when(g + 1 < pl.num_programs(0) * n_distinct)
            def _():
                wrap = (rank_ref[i] + 1 == n_distinct).astype(jnp.int32)
                for cp in weight_copies(next_ref[i], j + wrap, 1 - ws):
                    cp.start()

            for cp in weight_copies(be_ref[i], j, ws):
                cp.wait()

        def drain(r, _):
            _row_copy(x_hbm, 0, xbuf_ref.at[slot], r, sem.at[slot]).wait()
            return 0
        lax.fori_loop(0, nv_ref[i], drain, 0)
        xb_ref[...] = xbuf_ref[slot].astype(_BF16)
        a = jnp.zeros(o_ref.shape, _F32)
        b = jnp.zeros(o_ref.shape, _F32)
        for c0 in range(0, d, kc):
            x = xb_ref[:, c0:c0 + kc]
            a = a + _dot(x, w1buf_ref[ws, c0:c0 + kc, :].astype(_BF16))
            b = b + _dot(x, w3buf_ref[ws, c0:c0 + kc, :].astype(_BF16))
        o_ref[...] = (a * _sigmoid(a) * b).astype(o_ref.dtype)

    @pl.when(i >= n_active)
    def _():
        o_ref[...] = jnp.zeros_like(o_ref)


def _expert_changed(be_ref, i):
    prev = be_ref[jnp.maximum(i - 1, 0)]
    return (i == 0) | (be_ref[i] != prev)


def _gmm2_kernel(be_ref, na_ref, h_ref, w2_ref, o_ref, w2b_ref):
    i = pl.program_id(1)

    @pl.when(i < na_ref[0])
    def _():
        @pl.when(_expert_changed(be_ref, i))
        def _():
            w2b_ref[...] = w2_ref[...].astype(_BF16)

        o_ref[...] = _dot(h_ref[...], w2b_ref[...])

    @pl.when(i >= na_ref[0])
    def _():
        o_ref[...] = jnp.zeros_like(o_ref)


def _expert_ffn(x, row_tok, n_valid, block_e, n_active, w1, w3, w2, layer):
    n_rows = row_tok.shape[0]
    d = x.shape[1]
    f = w1.shape[-1]
    bm = _MOE_ROWS
    tn1 = _blk(f, 512)
    tn2 = _blk(d, 2048)
    idx = jnp.arange(n_rows // bm, dtype=jnp.int32)
    first = ((idx < n_active[0]) & ((idx == 0) | (block_e != jnp.roll(block_e, 1)))).astype(jnp.int32)
    rank = jnp.maximum(jnp.cumsum(first) - 1, 0).astype(jnp.int32)
    n_distinct = jnp.sum(first).astype(jnp.int32)
    distinct_e = jnp.zeros_like(block_e).at[rank].set(block_e)
    next_e = distinct_e[(rank + 1) % jnp.maximum(n_distinct, 1)]
    any_spec = pl.BlockSpec(memory_space=pl.ANY)
    hidden = pl.pallas_call(
        functools.partial(_moe_up_kernel, rows=bm, kc=_blk(d, 512), layer=layer),
        grid_spec=pltpu.PrefetchScalarGridSpec(
            num_scalar_prefetch=8, grid=(f // tn1, n_rows // bm),
            in_specs=[any_spec, any_spec, any_spec],
            out_specs=pl.BlockSpec((bm, tn1), lambda j, i, *_: (i, j)),
            scratch_shapes=[pltpu.VMEM((2, bm, d), x.dtype), pltpu.VMEM((bm, d), _BF16),
                            pltpu.VMEM((2, d, tn1), w1.dtype), pltpu.VMEM((2, d, tn1), w3.dtype),
                            pltpu.SemaphoreType.DMA((2,)), pltpu.SemaphoreType.DMA((2,))]),
        out_shape=jax.ShapeDtypeStruct((n_rows, f), _BF16),
        compiler_params=_params("arbitrary", "arbitrary"), name="moe_up")(
            block_e, n_active, first, rank, next_e, n_distinct.reshape(1), n_valid, row_tok, x, w1, w3)

    def last_active(i, na):
        return jnp.minimum(i, jnp.maximum(na[0] - 1, 0))

    return pl.pallas_call(
        _gmm2_kernel,
        grid_spec=pltpu.PrefetchScalarGridSpec(
            num_scalar_prefetch=2, grid=(d // tn2, n_rows // bm),
            in_specs=[pl.BlockSpec((bm, f), lambda j, i, be, na: (last_active(i, na), 0)),
                      pl.BlockSpec((None, None, f, tn2), lambda j, i, be, na: (layer, be[i], 0, j))],
            out_specs=pl.BlockSpec((bm, tn2), lambda j, i, be, na: (i, j)),
            scratch_shapes=[pltpu.VMEM((f, tn2), _BF16)]),
        out_shape=jax.ShapeDtypeStruct((n_rows, d), _F32),
        compiler_params=_params("arbitrary", "arbitrary"), name="moe_down")(block_e, n_active, hidden, w2)


def _combine_kernel(pos_ref, x_ref, r_ref, g_ref, b_ref, ys_hbm, o_ref, buf_ref, sem, *, bt, alpha):
    i = pl.program_id(0)
    slot = i % 2

    def start(blk, s):
        def body(r, _):
            for k in range(TOP_K):
                _row_copy(ys_hbm, pos_ref[(blk * bt + r) * TOP_K + k], buf_ref.at[s], k * bt + r, sem.at[s]).start()
            return 0
        lax.fori_loop(0, bt, body, 0, unroll=_DMA_UNROLL)

    @pl.when(i == 0)
    def _():
        start(0, 0)

    @pl.when(i + 1 < pl.num_programs(0))
    def _():
        start(i + 1, 1 - slot)

    def drain(r, _):
        _row_copy(ys_hbm, 0, buf_ref.at[slot], r, sem.at[slot]).wait()
        return 0

    lax.fori_loop(0, TOP_K * bt, drain, 0, unroll=_DMA_UNROLL)
    slab = r_ref[...]
    ffn = buf_ref[slot, 0:bt, :] * slab[:, 2:3]
    for k in range(1, TOP_K):
        ffn = ffn + buf_ref[slot, k * bt:(k + 1) * bt, :] * slab[:, 2 + k:3 + k]
    o_ref[...] = _layer_norm(alpha * x_ref[...] + ffn, g_ref[...], b_ref[...])


def _combine_ln(x1, row0, m, slab, pos, ys, g, b, alpha, name="moe_combine"):
    d = x1.shape[1]
    bt = _blk(m, 128)
    assert row0 % bt == 0
    vec = pl.BlockSpec((1, d), lambda i, pos: (0, 0))
    grid_spec = pltpu.PrefetchScalarGridSpec(
        num_scalar_prefetch=1, grid=(m // bt,),
        in_specs=[pl.BlockSpec((bt, d), lambda i, pos: (i + row0 // bt, 0)),
                  pl.BlockSpec((bt, _LANES), lambda i, pos: (i, 0)), vec, vec,
                  pl.BlockSpec(memory_space=pl.ANY)],
        out_specs=pl.BlockSpec((bt, d), lambda i, pos: (i, 0)),
        scratch_shapes=[pltpu.VMEM((2, TOP_K * bt, d), _F32), pltpu.SemaphoreType.DMA((2,))])
    return pl.pallas_call(
        functools.partial(_combine_kernel, bt=bt, alpha=alpha), grid_spec=grid_spec,
        out_shape=jax.ShapeDtypeStruct((m, d), _F32),
        compiler_params=_params("arbitrary"), name=name)(pos, x1, slab, g.reshape(1, d), b.reshape(1, d), ys)


def _dispatch_plan(expert, n_experts, bm):
    a = expert.shape[0]
    n_blocks = -(-(a + n_experts * (bm - 1)) // bm)
    n_rows = n_blocks * bm
    onehot = (expert[:, None] == jnp.arange(n_experts, dtype=jnp.int32)[None, :]).astype(jnp.int32)
    rank = jnp.take_along_axis(jnp.cumsum(onehot, axis=0) - onehot, expert[:, None], axis=1)[:, 0]
    counts = jnp.sum(onehot, axis=0)
    padded = (counts + bm - 1) // bm * bm
    pad_end = jnp.cumsum(padded)
    pad_start = pad_end - padded
    dest = (pad_start[expert] + rank).astype(jnp.int32)
    row_src = jnp.full((n_rows,), -1, jnp.int32).at[dest].set(jnp.arange(a, dtype=jnp.int32))
    n_active = (pad_end[-1] // bm).astype(jnp.int32)
    blk = jnp.minimum(jnp.arange(n_blocks, dtype=jnp.int32), jnp.maximum(n_active - 1, 0))
    block_e = jnp.minimum(jnp.searchsorted(pad_end, blk * bm, side="right"), n_experts - 1).astype(jnp.int32)
    return dest, row_src, block_e, n_active.reshape(1)


def _branches(x, w_in, w_att_out, w_conv_out, w_out, b_gate, attn_fn, conv_fn, widths, kv_bf16):
    a, c, d = widths
    scale = float(a // attn_fn.n_heads) ** -0.5 * _LOG2E
    act = x.dtype
    q = _project(x, w_in, 0, a, act, scale=scale, name="proj_q")
    if kv_bf16:
        k, k_b = _project_kv(x, w_in, a, a, "proj_k")
        v, v_b = _project_kv(x, w_in, 2 * a, a, "proj_v")
    else:
        k = k_b = _project(x, w_in, a, a, _F32, name="proj_k")
        v = v_b = _project(x, w_in, 2 * a, a, _F32, name="proj_v")
    u = _project_glu(x, w_in, 3 * a, c)
    gates = _project_gate(x, w_in, 3 * a + 2 * c, 2 * d, b_gate)
    att = attn_fn(q, k_b, v_b)
    cnv, conv_state = conv_fn(u)
    pre = _mix(att.astype(act), cnv.astype(act), w_att_out, w_conv_out, gates, act)
    mixed = _project(pre, w_out, 0, d, _F32, name="proj_out")
    return mixed, k, v, conv_state


class _PromptAttn:
    def __init__(self, bias, n_heads):
        self.bias, self.n_heads = bias, n_heads

    def __call__(self, q, k, v):
        return _attn_prompt(q, k, v, self.bias, self.n_heads)


class _SampleAttn:
    def __init__(self, bias, n_heads, cache_k, cache_v, page_table, layer):
        self.bias, self.n_heads = bias, n_heads
        self.cache_k, self.cache_v, self.page_table, self.layer = cache_k, cache_v, page_table, layer

    def __call__(self, q, k, v):
        b, a = q.shape
        out = _attn_sample(q.reshape(b, self.n_heads, a // self.n_heads), self.cache_k, self.cache_v,
                           self.page_table, self.bias, self.layer)
        return out.reshape(b, a)


def kernel(x_prompt, x_sample, cache_k, cache_v, state_conv, page_table, w_in, b_sb, b_gate, conv_w, conv_b,
           conv_ln_g, conv_ln_b, w_att_out, w_conv_out, w_out, ln1_g, ln1_b, w_router_group, b_router_group,
           w_router_expert, b_router_expert, w1, w3, w2, ln2_g, ln2_b):
    depth, d, _ = w_in.shape
    n_heads, d_head = cache_k.shape[3], cache_k.shape[4]
    a = n_heads * d_head
    c = conv_w.shape[2]
    taps = conv_w.shape[1]
    n_groups = w_router_group.shape[2]
    n_experts = w_router_expert.shape[2]
    per_group = n_experts // n_groups
    assert n_groups + n_experts <= _LANES
    alpha = (2.0 * depth) ** 0.25
    bsz, seq, _ = x_prompt.shape
    dec, dec_seq, _ = x_sample.shape
    assert bsz == 1 and dec_seq == 1
    n_p, n_s = bsz * seq, dec * dec_seq

    y_p = x_prompt.reshape(n_p, d)
    y_s = x_sample.reshape(n_s, d)
    outs = {k: [] for k in ("kp", "vp", "cp", "ks", "vs", "cs")}
    for l in range(depth):
        win, wao, wco, wo = w_in[l], w_att_out[l], w_conv_out[l], w_out[l]
        hist0 = jnp.zeros((taps - 1, c), _F32)
        conv_p = lambda u, l=l, hist0=hist0: (
            _conv_prompt(u, hist0, conv_w[l], conv_b[l], conv_ln_g[l], conv_ln_b[l], _BF16), u[seq - (taps - 1):])
        weights = (win.astype(_BF16), wao.astype(_BF16), wco.astype(_BF16), wo.astype(_BF16))
        mixed_p, k_p, v_p, cs_p = _branches(
            y_p.astype(_BF16), *weights, b_gate[l], _PromptAttn(b_sb[l], n_heads), conv_p, (a, c, d), True)
        conv_s = lambda u, l=l: _conv_sample(u, state_conv[l], conv_w[l], conv_b[l], conv_ln_g[l], conv_ln_b[l])
        mixed_s, k_s, v_s, cs_s = _branches(
            y_s.astype(_BF16), *weights, b_gate[l],
            _SampleAttn(b_sb[l], n_heads, cache_k, cache_v, page_table, l), conv_s, (a, c, d), False)

        n_all = n_p + -(-n_s // _LN_ROWS) * _LN_ROWS
        x1 = _res_ln(y_p, mixed_p, ln1_g[l], ln1_b[l], alpha, n_all, name="ln1_prompt")
        x1 = _res_ln(y_s, mixed_s, ln1_g[l], ln1_b[l], alpha, n_all, row0=n_p, into=x1, name="ln1_sample")

        w_r = jnp.zeros((d, _LANES), _F32).at[:, :n_groups].set(w_router_group[l]) \
                 .at[:, n_groups:n_groups + n_experts].set(w_router_expert[l])
        b_r = jnp.zeros((1, _LANES), _F32).at[0, :n_groups].set(b_router_group[l]) \
                 .at[0, n_groups:n_groups + n_experts].set(b_router_expert[l])
        slab_p = _router(x1, 0, n_p, w_r, b_r, n_groups, per_group, name="router_prompt")
        slab_s = _router(x1, n_p, n_s, w_r, b_r, n_groups, per_group, name="router_sample")
        expert = jnp.concatenate([slab_p[:, :TOP_K], slab_s[:, :TOP_K]], axis=0).astype(jnp.int32).reshape(-1)
        dest, row_src, block_e, n_active = _dispatch_plan(expert, n_experts, _MOE_ROWS)
        row_tok = jnp.maximum(row_src, 0) // TOP_K
        n_valid = jnp.sum((row_src >= 0).reshape(-1, _MOE_ROWS), axis=1).astype(jnp.int32)
        ys = _expert_ffn(x1, row_tok, n_valid, block_e, n_active, w1, w3, w2, l)
        y_p = _combine_ln(x1, 0, n_p, slab_p, dest[:n_p * TOP_K], ys, ln2_g[l], ln2_b[l], alpha,
                          name="combine_prompt")
        y_s = _combine_ln(x1, n_p, n_s, slab_s, dest[n_p * TOP_K:], ys, ln2_g[l], ln2_b[l], alpha,
                          name="combine_sample")

        outs["kp"].append(k_p.reshape(bsz, seq, n_heads, d_head))
        outs["vp"].append(v_p.reshape(bsz, seq, n_heads, d_head))
        outs["cp"].append(cs_p.reshape(bsz, taps - 1, c))
        outs["ks"].append(k_s.reshape(dec, dec_seq, n_heads, d_head))
        outs["vs"].append(v_s.reshape(dec, dec_seq, n_heads, d_head))
        outs["cs"].append(cs_s)
    return (y_p.reshape(bsz, seq, d), y_s.reshape(dec, dec_seq, d),
            jnp.stack(outs["kp"]), jnp.stack(outs["vp"]), jnp.stack(outs["cp"]),
            jnp.stack(outs["ks"]), jnp.stack(outs["vs"]), jnp.stack(outs["cs"]))
```

```python
import functools

import jax
import jax.numpy as jnp
from jax import lax
from jax.experimental import pallas as pl
from jax.experimental.pallas import tpu as pltpu

_F32 = jnp.float32
_BF16 = jnp.bfloat16

LN_EPS = 1e-5
TOP_K = 2
_VMEM_LIMIT_V7X = 56 * 1024 * 1024
_LANES = 128
_SUBLANES = 8
_MOE_ROWS = 256


def _params(*sem):
    return pltpu.CompilerParams(dimension_semantics=sem, vmem_limit_bytes=_VMEM_LIMIT_V7X)


def _blk(dim, pref):
    if dim <= pref:
        return dim
    assert dim % pref == 0, (dim, pref)
    return pref


def _split_bf16(x):
    hi = x.astype(_BF16)
    lo = (x - hi.astype(_F32)).astype(_BF16)
    return hi, lo


def _dot(x, w):
    assert x.dtype == w.dtype, (x.dtype, w.dtype)
    d = functools.partial(jnp.dot, preferred_element_type=_F32)
    if x.dtype == _BF16:
        return d(x, w)
    xh, xl = _split_bf16(x)
    wh, wl = _split_bf16(w)
    return d(xh, wh) + (d(xl, wh) + d(xh, wl))


def _sigmoid(x):
    return 1.0 / (1.0 + jnp.exp(-x))


def _layer_norm(v, g, b):
    mu = jnp.mean(v, axis=-1, keepdims=True)
    d = v - mu
    var = jnp.mean(d * d, axis=-1, keepdims=True)
    return d * lax.rsqrt(var + LN_EPS) * g + b


def _mm_kernel(x_ref, w_ref, o_ref, *, scale):
    acc = _dot(x_ref[...], w_ref[...])
    if scale != 1.0:
        acc = acc * scale
    o_ref[...] = acc.astype(o_ref.dtype)


def _mm2_kernel(x_ref, w_ref, o_ref, ob_ref):
    acc = _dot(x_ref[...], w_ref[...])
    o_ref[...] = acc
    ob_ref[...] = acc.astype(ob_ref.dtype)


def _glu_kernel(x_ref, wv_ref, wg_ref, o_ref):
    x = x_ref[...]
    o_ref[...] = _dot(x, wv_ref[...]) * _sigmoid(_dot(x, wg_ref[...]))


def _gate_kernel(x_ref, w_ref, b_ref, o_ref):
    o_ref[...] = _sigmoid(_dot(x_ref[...], w_ref[...]) + b_ref[...])


def _mix_kernel(a_ref, c_ref, wa_ref, wc_ref, ga_ref, gc_ref, o_ref):
    att = _dot(a_ref[...], wa_ref[...])
    cnv = _dot(c_ref[...], wc_ref[...])
    o_ref[...] = (ga_ref[...] * att + gc_ref[...] * cnv).astype(o_ref.dtype)


def _row_spec(bm, k):
    return pl.BlockSpec((bm, k), lambda j, i: (i, 0))


def _col_spec(k, bn, off_blocks=0):
    return pl.BlockSpec((k, bn), lambda j, i: (0, j + off_blocks))


def _tile_spec(bm, bn, off_blocks=0):
    return pl.BlockSpec((bm, bn), lambda j, i: (i, j + off_blocks))


def _dense_call(kernel, name, m, n, bm, bn, in_specs, out_dtypes, args):
    outs = [jax.ShapeDtypeStruct((m, n), dt) for dt in out_dtypes]
    specs = [_tile_spec(bm, bn) for _ in out_dtypes]
    single = len(outs) == 1
    return pl.pallas_call(
        kernel, grid=(n // bn, m // bm), in_specs=in_specs,
        out_specs=specs[0] if single else specs,
        out_shape=outs[0] if single else outs,
        compiler_params=_params("parallel", "parallel"), name=name)(*args)


def _dense_blocks(x, n, rows=512):
    m, k = x.shape
    prec3 = x.dtype == _F32
    bm = _blk(m, rows)
    bn = _blk(n, 256 if prec3 else 1024)
    return m, k, bm, bn


_WIDE_ROWS = 1024


def _project(x, w, col0, n, out_dtype, scale=1.0, name="proj"):
    m, k, bm, bn = _dense_blocks(x, n, _WIDE_ROWS)
    return _dense_call(functools.partial(_mm_kernel, scale=scale), name, m, n, bm, bn,
                       [_row_spec(bm, k), _col_spec(k, bn, col0 // bn)], [out_dtype], (x, w))


def _project_kv(x, w, col0, n, name):
    m, k, bm, bn = _dense_blocks(x, n, _WIDE_ROWS)
    return _dense_call(_mm2_kernel, name, m, n, bm, bn,
                       [_row_spec(bm, k), _col_spec(k, bn, col0 // bn)], [_F32, _BF16], (x, w))


def _project_glu(x, w, col0, c, name="glu"):
    m, k, bm, bn = _dense_blocks(x, c)
    return _dense_call(_glu_kernel, name, m, c, bm, bn,
                       [_row_spec(bm, k), _col_spec(k, bn, col0 // bn), _col_spec(k, bn, (col0 + c) // bn)],
                       [_F32], (x, w, w))


def _project_gate(x, w, col0, n, b_gate, name="gate"):
    m, k, bm, bn = _dense_blocks(x, n, _WIDE_ROWS)
    b_spec = pl.BlockSpec((1, bn), lambda j, i: (0, j))
    return _dense_call(_gate_kernel, name, m, n, bm, bn,
                       [_row_spec(bm, k), _col_spec(k, bn, col0 // bn), b_spec], [_F32],
                       (x, w, b_gate.reshape(1, n)))


def _mix(att, cnv, w_att, w_cnv, gates, out_dtype, name="mix"):
    n = w_att.shape[1]
    m, ka, bm, bn = _dense_blocks(att, n)
    kc = cnv.shape[1]
    return _dense_call(_mix_kernel, name, m, n, bm, bn,
                       [_row_spec(bm, ka), _row_spec(bm, kc), _col_spec(ka, bn), _col_spec(kc, bn),
                        _tile_spec(bm, bn), _tile_spec(bm, bn, n // bn)],
                       [out_dtype], (att, cnv, w_att, w_cnv, gates, gates))


_LN_ROWS = 256


def _res_ln_kernel(x_ref, r_ref, g_ref, b_ref, *rest, alpha, n_blocks):
    o_ref = rest[-1]
    i = pl.program_id(0)

    @pl.when(i < n_blocks)
    def _():
        o_ref[...] = _layer_norm(alpha * x_ref[...] + r_ref[...], g_ref[...], b_ref[...])

    @pl.when(i >= n_blocks)
    def _():
        o_ref[...] = jnp.zeros_like(o_ref)


def _res_ln(x, r, g, b, alpha, total_rows, row0=0, into=None, name="res_ln"):
    m, d = x.shape
    bt = _blk(m, _LN_ROWS)
    assert row0 % bt == 0 and total_rows % bt == 0
    n_blocks = m // bt
    row = pl.BlockSpec((bt, d), lambda i: (jnp.minimum(i, n_blocks - 1), 0))
    vec = pl.BlockSpec((1, d), lambda i: (0, 0))
    args = [x, r, g.reshape(1, d), b.reshape(1, d)]
    in_specs = [row, row, vec, vec]
    aliases = {}
    grid = total_rows // bt
    if into is not None:
        assert into.shape == (total_rows, d)
        args.append(into)
        in_specs.append(pl.BlockSpec(memory_space=pl.ANY))
        aliases = {4: 0}
        grid = n_blocks
    else:
        assert row0 == 0
    return pl.pallas_call(
        functools.partial(_res_ln_kernel, alpha=alpha, n_blocks=n_blocks), grid=(grid,),
        in_specs=in_specs, out_specs=pl.BlockSpec((bt, d), lambda i: (i + row0 // bt, 0)),
        out_shape=jax.ShapeDtypeStruct((total_rows, d), _F32), input_output_aliases=aliases,
        compiler_params=_params("parallel"), name=name)(*args)


def _router_kernel(x_ref, w_ref, b_ref, o_ref, *, n_groups, per_group):
    logits = _dot(x_ref[...], w_ref[...]) + b_ref[...]
    lane = lax.broadcasted_iota(jnp.int32, logits.shape, 1).astype(_F32)
    neg = -jnp.inf
    big = 1e9

    def arg_max(v):
        m = jnp.max(v, axis=-1, keepdims=True)
        return m, jnp.min(jnp.where(v == m, lane, big), axis=-1, keepdims=True)

    glog = jnp.where(lane < n_groups, logits, neg)
    gmax, g_idx = arg_max(glog)
    g_p = 1.0 / jnp.sum(jnp.exp(glog - gmax), axis=-1, keepdims=True)
    lo = n_groups + g_idx * per_group
    elog = jnp.where((lane >= lo) & (lane < lo + per_group), logits, neg)
    m1, i1 = arg_max(elog)
    m2, i2 = arg_max(jnp.where(lane == i1, neg, elog))
    e2 = jnp.exp(m2 - m1)
    p1 = 1.0 / (1.0 + e2)
    p2 = e2 / (1.0 + e2)
    o_ref[...] = jnp.where(lane == 0, i1 - n_groups,
                 jnp.where(lane == 1, i2 - n_groups,
                 jnp.where(lane == 2, g_p * p1,
                 jnp.where(lane == 3, g_p * p2, 0.0))))


def _router(x1, row0, m, w_r, b_r, n_groups, per_group, name="router"):
    d = x1.shape[1]
    bt = _blk(m, 256)
    assert row0 % bt == 0
    return pl.pallas_call(
        functools.partial(_router_kernel, n_groups=n_groups, per_group=per_group), grid=(m // bt,),
        in_specs=[pl.BlockSpec((bt, d), lambda i: (i + row0 // bt, 0)),
                  pl.BlockSpec((d, _LANES), lambda i: (0, 0)),
                  pl.BlockSpec((1, _LANES), lambda i: (0, 0))],
        out_specs=pl.BlockSpec((bt, _LANES), lambda i: (i, 0)),
        out_shape=jax.ShapeDtypeStruct((m, _LANES), _F32),
        compiler_params=_params("parallel"), name=name)(x1, w_r, b_r)


_LOG2E = 1.4426950408889634
_SAMPLE_PAGES_PER_STEP = 8


def _neg_abs(z):
    return lax.bitcast_convert_type(lax.bitcast_convert_type(z, jnp.uint32) | jnp.uint32(0x80000000), _F32)


def _neg_softplus2(z):
    return jnp.maximum(z, 0.0) + jnp.log2(1.0 + jnp.exp2(_neg_abs(z)))


_MASKED_LOGIT = -1e30


def _neg_suffix_matrix(n, classes=1, copies=2):
    r = lax.broadcasted_iota(jnp.int32, (copies * n, n), 0) & (n - 1)
    c = lax.broadcasted_iota(jnp.int32, (copies * n, n), 1)
    keep = r >= c
    if classes > 1:
        keep = keep & ((r & (classes - 1)) == (c & (classes - 1)))
    return jnp.where(keep, -1.0, 0.0).astype(_BF16)


def _attn_prompt_kernel(bias_ref, q_ref, k_ref, v_ref, o_ref, acc_ref, carry_ref, z0_ref, z1_ref, s0_ref, s1_ref,
                        *, bq, sub):
    h = pl.program_id(0)
    qi = pl.program_id(1)
    bias = bias_ref[h]
    nsub = bq // sub
    neg_suffix = _neg_suffix_matrix(sub, copies=1)
    acc_ref[...] = jnp.zeros_like(acc_ref)
    carry_ref[...] = jnp.zeros_like(carry_ref)

    def front(j, z_ref, s_ref, masked=False):
        k0 = pl.multiple_of(j * bq, bq)
        z = lax.dot_general(q_ref[...], k_ref[pl.ds(k0, bq), :], (((1,), (1,)), ((), ())),
                            preferred_element_type=_F32) + bias
        if masked:
            q_pos = qi * bq + lax.broadcasted_iota(jnp.int32, (bq, bq), 0)
            k_pos = k0 + lax.broadcasted_iota(jnp.int32, (bq, bq), 1)
            z = jnp.where(k_pos < q_pos, z, _MASKED_LOGIT)
        z_ref[...] = z
        for s in range(nsub):
            s_ref[s * bq:(s + 1) * bq, :] = _neg_softplus2(z[:, s * sub:(s + 1) * sub]).astype(_BF16)

    def back(j, z_ref, s_ref):
        vb = v_ref[pl.ds(pl.multiple_of(j * bq, bq), bq), :]
        incl = jnp.dot(s_ref[...], neg_suffix, preferred_element_type=_F32)
        carry = carry_ref[...]
        ws = [None] * nsub
        for s in range(nsub - 1, -1, -1):
            inc = incl[s * bq:(s + 1) * bq, :]
            ws[s] = jnp.exp2(z_ref[:, s * sub:(s + 1) * sub] + inc + carry).astype(_BF16)
            carry = carry + inc[:, 0:1]
        carry_ref[...] = carry
        acc_ref[...] += jnp.dot(jnp.concatenate(ws, axis=1), vb, preferred_element_type=_F32)

    front(qi, z0_ref, s0_ref, masked=True)

    def pair(t, _):
        c = qi - 2 * t
        front(c - 1, z1_ref, s1_ref)
        back(c, z0_ref, s0_ref)
        front(c - 2, z0_ref, s0_ref)
        back(c - 1, z1_ref, s1_ref)
        return 0

    lax.fori_loop(0, qi // 2, pair, 0)

    @pl.when(qi % 2 == 1)
    def _():
        front(0, z1_ref, s1_ref)
        back(1, z0_ref, s0_ref)
        back(0, z1_ref, s1_ref)

    @pl.when(qi % 2 == 0)
    def _():
        back(0, z0_ref, s0_ref)

    o_ref[...] = acc_ref[...].astype(o_ref.dtype)


def _attn_prompt(q, k, v, bias, n_heads, name="attn_prompt"):
    t, a = q.shape
    dh = a // n_heads
    bq = _blk(t, 512)
    sub = _blk(bq, 256)
    assert sub & (sub - 1) == 0
    bias = bias * _LOG2E
    kernel = functools.partial(_attn_prompt_kernel, bq=bq, sub=sub)
    return pl.pallas_call(
        kernel, grid=(n_heads, t // bq),
        in_specs=[pl.BlockSpec(memory_space=pltpu.SMEM),
                  pl.BlockSpec((bq, dh), lambda h, i: (i, h)),
                  pl.BlockSpec((t, dh), lambda h, i: (0, h)),
                  pl.BlockSpec((t, dh), lambda h, i: (0, h))],
        out_specs=pl.BlockSpec((bq, dh), lambda h, i: (i, h)),
        out_shape=jax.ShapeDtypeStruct((t, a), _BF16),
        scratch_shapes=[pltpu.VMEM((bq, dh), _F32), pltpu.VMEM((bq, 1), _F32),
                        pltpu.VMEM((bq, bq), _F32), pltpu.VMEM((bq, bq), _F32),
                        pltpu.VMEM((bq * bq // sub, sub), _BF16), pltpu.VMEM((bq * bq // sub, sub), _BF16)],
        compiler_params=_params("parallel", "parallel"), name=name)(bias, q, k, v)


def _attn_sample_kernel(pt_ref, q_ref, bias_ref, *refs, n_heads, n_steps, pages):
    k_refs, v_refs = refs[:pages], refs[pages:2 * pages]
    o_ref, acc_ref, carry_ref = refs[2 * pages:]
    p = pl.program_id(1)
    hh = n_heads
    cols = k_refs[0].shape[0]
    nblk = cols // _LANES

    @pl.when(p == 0)
    def _():
        acc_ref[...] = jnp.zeros_like(acc_ref)
        carry_ref[...] = jnp.zeros_like(carry_ref)

    q = q_ref[...].astype(_BF16)
    bias = bias_ref[...]
    tot_r = lax.broadcasted_iota(jnp.int32, (2 * _LANES, _LANES), 0)
    tot_c = lax.broadcasted_iota(jnp.int32, (2 * _LANES, _LANES), 1)
    neg_total = jnp.where((tot_r & (hh - 1)) == (tot_c & (hh - 1)), -1.0, 0.0).astype(_BF16)
    sums_mat = jnp.concatenate([_neg_suffix_matrix(_LANES, classes=hh), neg_total], axis=1)
    row = lax.broadcasted_iota(jnp.int32, (hh, cols), 0)
    col = lax.broadcasted_iota(jnp.int32, (hh, cols), 1)
    real = (col & (hh - 1)) == row

    suffix = carry_ref[...]
    acc = acc_ref[...]
    for s in range(pages):
        kp = k_refs[s][...].astype(_BF16)
        vp = v_refs[s][...].astype(_BF16)
        z = lax.dot_general(q, kp, (((1,), (1,)), ((), ())), preferred_element_type=_F32) + bias
        sp = _neg_softplus2(z)
        stacked = jnp.concatenate([sp[:, i * _LANES:(i + 1) * _LANES] for i in range(nblk)], axis=0)
        hi, lo = _split_bf16(stacked)
        sums = jnp.dot(jnp.concatenate([hi, lo], axis=1), sums_mat, preferred_element_type=_F32)
        later = [None] * nblk
        for i in range(nblk - 1, -1, -1):
            later[i] = sums[i * hh:(i + 1) * hh, :_LANES] + suffix
            suffix = suffix + sums[i * hh:(i + 1) * hh, _LANES:]
        w = jnp.exp2(z + jnp.concatenate(later, axis=1))
        acc = acc + jnp.dot(jnp.where(real, w, 0.0).astype(_BF16), vp, preferred_element_type=_F32)
    carry_ref[...] = suffix
    acc_ref[...] = acc

    @pl.when(p == n_steps - 1)
    def _():
        o_ref[...] = acc


def _attn_sample(q, cache_k, cache_v, page_table, bias, layer, name="attn_sample"):
    b, hh, dh = q.shape
    assert hh & (hh - 1) == 0 and _LANES % hh == 0
    depth, n_phys, page = cache_k.shape[:3]
    n_pages = page_table.shape[1]
    ck = cache_k.reshape(depth, n_phys, page * hh, dh)
    cv = cache_v.reshape(depth, n_phys, page * hh, dh)
    pages = _SAMPLE_PAGES_PER_STEP if n_pages % _SAMPLE_PAGES_PER_STEP == 0 else 1
    n_steps = n_pages // pages

    def page_spec(s):
        return pl.BlockSpec((None, None, page * hh, dh),
                            lambda i, p, pt: (layer, pt[i * n_pages + (n_pages - 1 - (p * pages + s))], 0, 0))

    page_specs = [page_spec(s) for s in range(pages)]
    grid_spec = pltpu.PrefetchScalarGridSpec(
        num_scalar_prefetch=1, grid=(b, n_steps),
        in_specs=[pl.BlockSpec((None, hh, dh), lambda i, p, pt: (i, 0, 0)),
                  pl.BlockSpec((hh, 1), lambda i, p, pt: (0, 0))] + page_specs + page_specs,
        out_specs=pl.BlockSpec((None, hh, dh), lambda i, p, pt: (i, 0, 0)),
        scratch_shapes=[pltpu.VMEM((hh, dh), _F32), pltpu.VMEM((hh, _LANES), _F32)])
    return pl.pallas_call(
        functools.partial(_attn_sample_kernel, n_heads=hh, n_steps=n_steps, pages=pages),
        grid_spec=grid_spec, out_shape=jax.ShapeDtypeStruct((b, hh, dh), _F32),
        compiler_params=_params("parallel", "arbitrary"), name=name)(
            page_table.reshape(-1), q, (bias * _LOG2E).reshape(hh, 1), *([ck] * pages), *([cv] * pages))


_HALO = 32


def _conv_prompt_kernel(u_ref, prev_ref, hist_ref, w_ref, b_ref, g_ref, beta_ref, o_ref, win_ref, y_ref,
                        *, taps, tb, cw):
    i = pl.program_id(0)
    bt, ch = u_ref.shape
    lead = _HALO - (taps - 1)
    win_ref[0:_HALO, :] = jnp.where(i == 0, hist_ref[...], prev_ref[...])
    win_ref[_HALO:, :] = u_ref[...]
    by_residue = [[k for k in range(taps) if (k + lead) % _SUBLANES == res] for res in range(_SUBLANES)]
    for c0 in range(0, ch, cw):
        for t0 in range(0, bt, tb):
            acc = jnp.zeros((tb // _SUBLANES, _SUBLANES, cw), _F32)
            for res, ks in enumerate(by_residue):
                if not ks:
                    continue
                reach = ks[-1] + lead - res
                shifted = win_ref[t0 + res:t0 + res + tb + reach, c0:c0 + cw]
                shifted = shifted.reshape((tb + reach) // _SUBLANES, _SUBLANES, cw)
                for k in ks:
                    a0 = (k + lead - res) // _SUBLANES
                    acc = acc + shifted[a0:a0 + tb // _SUBLANES] * w_ref[k, :, c0:c0 + cw][None]
            y_ref[t0:t0 + tb, c0:c0 + cw] = acc.reshape(tb, cw)
    y = _layer_norm(y_ref[...] + b_ref[...], g_ref[...], beta_ref[...])
    o_ref[...] = (y * _sigmoid(y)).astype(o_ref.dtype)


def _conv_prompt(u, hist, conv_w, conv_b, ln_g, ln_b, out_dtype, name="conv_prompt"):
    t, ch = u.shape
    taps = conv_w.shape[0]
    assert taps - 1 <= _HALO and t % _HALO == 0
    bt = _blk(t, 128)
    hist_pad = jnp.concatenate([jnp.zeros((_HALO - (taps - 1), ch), u.dtype), hist.astype(u.dtype)], axis=0)
    ratio = bt // _HALO
    vec = pl.BlockSpec((1, ch), lambda i: (0, 0))
    kernel = functools.partial(_conv_prompt_kernel, taps=taps, tb=min(bt, 64), cw=min(ch, 256))
    return pl.pallas_call(
        kernel, grid=(t // bt,),
        in_specs=[pl.BlockSpec((bt, ch), lambda i: (i, 0)),
                  pl.BlockSpec((_HALO, ch), lambda i: (jnp.maximum(i * ratio - 1, 0), 0)),
                  pl.BlockSpec((_HALO, ch), lambda i: (0, 0)),
                  pl.BlockSpec((taps, _SUBLANES, ch), lambda i: (0, 0, 0)), vec, vec, vec],
        out_specs=pl.BlockSpec((bt, ch), lambda i: (i, 0)),
        out_shape=jax.ShapeDtypeStruct((t, ch), out_dtype),
        scratch_shapes=[pltpu.VMEM((bt + _HALO, ch), _F32), pltpu.VMEM((bt, ch), _F32)],
        compiler_params=_params("parallel"), name=name)(
            u, u, hist_pad, jnp.broadcast_to(conv_w[:, None, :], (taps, _SUBLANES, ch)),
            conv_b.reshape(1, ch), ln_g.reshape(1, ch), ln_b.reshape(1, ch))


def _conv_sample_kernel(u_ref, s_ref, w_ref, b_ref, g_ref, beta_ref, o_ref, ns_ref, *, taps):
    s = s_ref[...]
    u = u_ref[...]
    conv = jnp.sum(s * w_ref[0:taps - 1, :], axis=0, keepdims=True) + u * w_ref[taps - 1:taps, :]
    y = _layer_norm(conv + b_ref[...], g_ref[...], beta_ref[...])
    o_ref[...] = y * _sigmoid(y)
    ns_ref[0:taps - 2, :] = s_ref[1:taps - 1, :]
    ns_ref[taps - 2:taps - 1, :] = u


def _conv_sample(u, state, conv_w, conv_b, ln_g, ln_b, name="conv_sample"):
    b, ch = u.shape
    taps = conv_w.shape[0]
    vec = pl.BlockSpec((1, ch), lambda i: (0, 0))
    row = pl.BlockSpec((None, 1, ch), lambda i: (i, 0, 0))
    st = pl.BlockSpec((None, taps - 1, ch), lambda i: (i, 0, 0))
    out, new_state = pl.pallas_call(
        functools.partial(_conv_sample_kernel, taps=taps), grid=(b,),
        in_specs=[row, st, pl.BlockSpec((taps, ch), lambda i: (0, 0)), vec, vec, vec],
        out_specs=[row, st],
        out_shape=[jax.ShapeDtypeStruct((b, 1, ch), _F32), jax.ShapeDtypeStruct((b, taps - 1, ch), _F32)],
        compiler_params=_params("parallel"), name=name)(
            u.reshape(b, 1, ch), state, conv_w, conv_b.reshape(1, ch), ln_g.reshape(1, ch), ln_b.reshape(1, ch))
    return out.reshape(b, ch), new_state


_DMA_UNROLL = 8


def _row_copy(src_hbm, src_row, dst_ref, dst_row, sem):
    return pltpu.make_async_copy(src_hbm.at[pl.ds(src_row, 1), :], dst_ref.at[pl.ds(dst_row, 1), :], sem)


def _moe_up_kernel(be_ref, na_ref, first_ref, rank_ref, next_ref, nd_ref, nv_ref, tok_ref, x_hbm, w1_hbm, w3_hbm,
                   o_ref, xbuf_ref, xb_ref, w1buf_ref, w3buf_ref, sem, wsem, *, rows, kc, layer):
    j = pl.program_id(0)
    i = pl.program_id(1)
    n_active = na_ref[0]
    slot = i % 2
    d, tn = w1buf_ref.shape[1], w1buf_ref.shape[2]

    def start(blk, s):
        def body(r, _):
            _row_copy(x_hbm, tok_ref[blk * rows + r], xbuf_ref.at[s], r, sem.at[s]).start()
            return 0
        lax.fori_loop(0, nv_ref[blk], body, 0)

    @pl.when((j == 0) & (i == 0))
    def _():
        xbuf_ref[...] = jnp.zeros_like(xbuf_ref)

    def weight_copies(e, col, s):
        cols = pl.ds(pl.multiple_of(col * tn, tn), tn)
        return (pltpu.make_async_copy(w1_hbm.at[layer, e, :, cols], w1buf_ref.at[s], wsem.at[s]),
                pltpu.make_async_copy(w3_hbm.at[layer, e, :, cols], w3buf_ref.at[s], wsem.at[s]))

    @pl.when((i == 0) & (n_active > 0))
    def _():
        start(0, 0)

    @pl.when(i + 1 < n_active)
    def _():
        start(i + 1, 1 - slot)

    @pl.when(i < n_active)
    def _():
        n_distinct = nd_ref[0]
        g = j * n_distinct + rank_ref[i]
        ws = g % 2

        @pl.when(first_ref[i] == 1)
        def _():
            @pl.when(g == 0)
            def _():
                for cp in weight_copies(be_ref[i], j, ws):
                    cp.start()

            @pl.when(g + 1 < pl.num_programs(0) * n_distinct)
            def _():
                wrap = (rank_ref[i] + 1 == n_distinct).astype(jnp.int32)
                for cp in weight_copies(next_ref[i], j + wrap, 1 - ws):
                    cp.start()

            for cp in weight_copies(be_ref[i], j, ws):
                cp.wait()

        def drain(r, _):
            _row_copy(x_hbm, 0, xbuf_ref.at[slot], r, sem.at[slot]).wait()
            return 0
        lax.fori_loop(0, nv_ref[i], drain, 0)
        xb_ref[...] = xbuf_ref[slot].astype(_BF16)
        a = jnp.zeros(o_ref.shape, _F32)
        b = jnp.zeros(o_ref.shape, _F32)
        for c0 in range(0, d, kc):
            x = xb_ref[:, c0:c0 + kc]
            a = a + _dot(x, w1buf_ref[ws, c0:c0 + kc, :].astype(_BF16))
            b = b + _dot(x, w3buf_ref[ws, c0:c0 + kc, :].astype(_BF16))
        o_ref[...] = (a * _sigmoid(a) * b).astype(o_ref.dtype)

    @pl.when(i >= n_active)
    def _():
        o_ref[...] = jnp.zeros_like(o_ref)


def _expert_changed(be_ref, i):
    prev = be_ref[jnp.maximum(i - 1, 0)]
    return (i == 0) | (be_ref[i] != prev)


def _gmm2_kernel(be_ref, na_ref, h_ref, w2_ref, o_ref, w2b_ref):
    i = pl.program_id(1)

    @pl.when(i < na_ref[0])
    def _():
        @pl.when(_expert_changed(be_ref, i))
        def _():
            w2b_ref[...] = w2_ref[...].astype(_BF16)

        o_ref[...] = _dot(h_ref[...], w2b_ref[...])

    @pl.when(i >= na_ref[0])
    def _():
        o_ref[...] = jnp.zeros_like(o_ref)


def _expert_ffn(x, row_tok, n_valid, block_e, n_active, w1, w3, w2, layer):
    n_rows = row_tok.shape[0]
    d = x.shape[1]
    f = w1.shape[-1]
    bm = _MOE_ROWS
    tn1 = _blk(f, 512)
    tn2 = _blk(d, 2048)
    idx = jnp.arange(n_rows // bm, dtype=jnp.int32)
    first = ((idx < n_active[0]) & ((idx == 0) | (block_e != jnp.roll(block_e, 1)))).astype(jnp.int32)
    rank = jnp.maximum(jnp.cumsum(first) - 1, 0).astype(jnp.int32)
    n_distinct = jnp.sum(first).astype(jnp.int32)
    distinct_e = jnp.zeros_like(block_e).at[rank].set(block_e)
    next_e = distinct_e[(rank + 1) % jnp.maximum(n_distinct, 1)]
    any_spec = pl.BlockSpec(memory_space=pl.ANY)
    hidden = pl.pallas_call(
        functools.partial(_moe_up_kernel, rows=bm, kc=_blk(d, 512), layer=layer),
        grid_spec=pltpu.PrefetchScalarGridSpec(
            num_scalar_prefetch=8, grid=(f // tn1, n_rows // bm),
            in_specs=[any_spec, any_spec, any_spec],
            out_specs=pl.BlockSpec((bm, tn1), lambda j, i, *_: (i, j)),
            scratch_shapes=[pltpu.VMEM((2, bm, d), x.dtype), pltpu.VMEM((bm, d), _BF16),
                            pltpu.VMEM((2, d, tn1), w1.dtype), pltpu.VMEM((2, d, tn1), w3.dtype),
                            pltpu.SemaphoreType.DMA((2,)), pltpu.SemaphoreType.DMA((2,))]),
        out_shape=jax.ShapeDtypeStruct((n_rows, f), _BF16),
        compiler_params=_params("arbitrary", "arbitrary"), name="moe_up")(
            block_e, n_active, first, rank, next_e, n_distinct.reshape(1), n_valid, row_tok, x, w1, w3)

    def last_active(i, na):
        return jnp.minimum(i, jnp.maximum(na[0] - 1, 0))

    return pl.pallas_call(
        _gmm2_kernel,
        grid_spec=pltpu.PrefetchScalarGridSpec(
            num_scalar_prefetch=2, grid=(d // tn2, n_rows // bm),
            in_specs=[pl.BlockSpec((bm, f), lambda j, i, be, na: (last_active(i, na), 0)),
                      pl.BlockSpec((None, None, f, tn2), lambda j, i, be, na: (layer, be[i], 0, j))],
            out_specs=pl.BlockSpec((bm, tn2), lambda j, i, be, na: (i, j)),
            scratch_shapes=[pltpu.VMEM((f, tn2), _BF16)]),
        out_shape=jax.ShapeDtypeStruct((n_rows, d), _F32),
        compiler_params=_params("arbitrary", "arbitrary"), name="moe_down")(block_e, n_active, hidden, w2)


def _combine_kernel(pos_ref, x_ref, r_ref, g_ref, b_ref, ys_hbm, o_ref, buf_ref, sem, *, bt, alpha):
    i = pl.program_id(0)
    slot = i % 2

    def start(blk, s):
        def body(r, _):
            for k in range(TOP_K):
                _row_copy(ys_hbm, pos_ref[(blk * bt + r) * TOP_K + k], buf_ref.at[s], k * bt + r, sem.at[s]).start()
            return 0
        lax.fori_loop(0, bt, body, 0, unroll=_DMA_UNROLL)

    @pl.when(i == 0)
    def _():
        start(0, 0)

    @pl.when(i + 1 < pl.num_programs(0))
    def _():
        start(i + 1, 1 - slot)

    def drain(r, _):
        _row_copy(ys_hbm, 0, buf_ref.at[slot], r, sem.at[slot]).wait()
        return 0

    lax.fori_loop(0, TOP_K * bt, drain, 0, unroll=_DMA_UNROLL)
    slab = r_ref[...]
    ffn = buf_ref[slot, 0:bt, :] * slab[:, 2:3]
    for k in range(1, TOP_K):
        ffn = ffn + buf_ref[slot, k * bt:(k + 1) * bt, :] * slab[:, 2 + k:3 + k]
    o_ref[...] = _layer_norm(alpha * x_ref[...] + ffn, g_ref[...], b_ref[...])


def _combine_ln(x1, row0, m, slab, pos, ys, g, b, alpha, name="moe_combine"):
    d = x1.shape[1]
    bt = _blk(m, 128)
    assert row0 % bt == 0
    vec = pl.BlockSpec((1, d), lambda i, pos: (0, 0))
    grid_spec = pltpu.PrefetchScalarGridSpec(
        num_scalar_prefetch=1, grid=(m // bt,),
        in_specs=[pl.BlockSpec((bt, d), lambda i, pos: (i + row0 // bt, 0)),
                  pl.BlockSpec((bt, _LANES), lambda i, pos: (i, 0)), vec, vec,
                  pl.BlockSpec(memory_space=pl.ANY)],
        out_specs=pl.BlockSpec((bt, d), lambda i, pos: (i, 0)),
        scratch_shapes=[pltpu.VMEM((2, TOP_K * bt, d), _F32), pltpu.SemaphoreType.DMA((2,))])
    return pl.pallas_call(
        functools.partial(_combine_kernel, bt=bt, alpha=alpha), grid_spec=grid_spec,
        out_shape=jax.ShapeDtypeStruct((m, d), _F32),
        compiler_params=_params("arbitrary"), name=name)(pos, x1, slab, g.reshape(1, d), b.reshape(1, d), ys)


def _dispatch_plan(expert, n_experts, bm):
    a = expert.shape[0]
    n_blocks = -(-(a + n_experts * (bm - 1)) // bm)
    n_rows = n_blocks * bm
    onehot = (expert[:, None] == jnp.arange(n_experts, dtype=jnp.int32)[None, :]).astype(jnp.int32)
    rank = jnp.take_along_axis(jnp.cumsum(onehot, axis=0) - onehot, expert[:, None], axis=1)[:, 0]
    counts = jnp.sum(onehot, axis=0)
    padded = (counts + bm - 1) // bm * bm
    pad_end = jnp.cumsum(padded)
    pad_start = pad_end - padded
    dest = (pad_start[expert] + rank).astype(jnp.int32)
    row_src = jnp.full((n_rows,), -1, jnp.int32).at[dest].set(jnp.arange(a, dtype=jnp.int32))
    n_active = (pad_end[-1] // bm).astype(jnp.int32)
    blk = jnp.minimum(jnp.arange(n_blocks, dtype=jnp.int32), jnp.maximum(n_active - 1, 0))
    block_e = jnp.minimum(jnp.searchsorted(pad_end, blk * bm, side="right"), n_experts - 1).astype(jnp.int32)
    return dest, row_src, block_e, n_active.reshape(1)


def _branches(x, w_in, w_att_out, w_conv_out, w_out, b_gate, attn_fn, conv_fn, widths, kv_bf16):
    a, c, d = widths
    scale = float(a // attn_fn.n_heads) ** -0.5 * _LOG2E
    act = x.dtype
    q = _project(x, w_in, 0, a, act, scale=scale, name="proj_q")
    if kv_bf16:
        k, k_b = _project_kv(x, w_in, a, a, "proj_k")
        v, v_b = _project_kv(x, w_in, 2 * a, a, "proj_v")
    else:
        k = k_b = _project(x, w_in, a, a, _F32, name="proj_k")
        v = v_b = _project(x, w_in, 2 * a, a, _F32, name="proj_v")
    u = _project_glu(x, w_in, 3 * a, c)
    gates = _project_gate(x, w_in, 3 * a + 2 * c, 2 * d, b_gate)
    att = attn_fn(q, k_b, v_b)
    cnv, conv_state = conv_fn(u)
    pre = _mix(att.astype(act), cnv.astype(act), w_att_out, w_conv_out, gates, act)
    mixed = _project(pre, w_out, 0, d, _F32, name="proj_out")
    return mixed, k, v, conv_state


class _PromptAttn:
    def __init__(self, bias, n_heads):
        self.bias, self.n_heads = bias, n_heads

    def __call__(self, q, k, v):
        return _attn_prompt(q, k, v, self.bias, self.n_heads)


class _SampleAttn:
    def __init__(self, bias, n_heads, cache_k, cache_v, page_table, layer):
        self.bias, self.n_heads = bias, n_heads
        self.cache_k, self.cache_v, self.page_table, self.layer = cache_k, cache_v, page_table, layer

    def __call__(self, q, k, v):
        b, a = q.shape
        out = _attn_sample(q.reshape(b, self.n_heads, a // self.n_heads), self.cache_k, self.cache_v,
                           self.page_table, self.bias, self.layer)
        return out.reshape(b, a)


def kernel(x_prompt, x_sample, cache_k, cache_v, state_conv, page_table, w_in, b_sb, b_gate, conv_w, conv_b,
           conv_ln_g, conv_ln_b, w_att_out, w_conv_out, w_out, ln1_g, ln1_b, w_router_group, b_router_group,
           w_router_expert, b_router_expert, w1, w3, w2, ln2_g, ln2_b):
    depth, d, _ = w_in.shape
    n_heads, d_head = cache_k.shape[3], cache_k.shape[4]
    a = n_heads * d_head
    c = conv_w.shape[2]
    taps = conv_w.shape[1]
    n_groups = w_router_group.shape[2]
    n_experts = w_router_expert.shape[2]
    per_group = n_experts // n_groups
    assert n_groups + n_experts <= _LANES
    alpha = (2.0 * depth) ** 0.25
    bsz, seq, _ = x_prompt.shape
    dec, dec_seq, _ = x_sample.shape
    assert bsz == 1 and dec_seq == 1
    n_p, n_s = bsz * seq, dec * dec_seq

    y_p = x_prompt.reshape(n_p, d)
    y_s = x_sample.reshape(n_s, d)
    outs = {k: [] for k in ("kp", "vp", "cp", "ks", "vs", "cs")}
    for l in range(depth):
        win, wao, wco, wo = w_in[l], w_att_out[l], w_conv_out[l], w_out[l]
        hist0 = jnp.zeros((taps - 1, c), _F32)
        conv_p = lambda u, l=l, hist0=hist0: (
            _conv_prompt(u, hist0, conv_w[l], conv_b[l], conv_ln_g[l], conv_ln_b[l], _BF16), u[seq - (taps - 1):])
        weights = (win.astype(_BF16), wao.astype(_BF16), wco.astype(_BF16), wo.astype(_BF16))
        mixed_p, k_p, v_p, cs_p = _branches(
            y_p.astype(_BF16), *weights, b_gate[l], _PromptAttn(b_sb[l], n_heads), conv_p, (a, c, d), True)
        conv_s = lambda u, l=l: _conv_sample(u, state_conv[l], conv_w[l], conv_b[l], conv_ln_g[l], conv_ln_b[l])
        mixed_s, k_s, v_s, cs_s = _branches(
            y_s.astype(_BF16), *weights, b_gate[l],
            _SampleAttn(b_sb[l], n_heads, cache_k, cache_v, page_table, l), conv_s, (a, c, d), False)

        n_all = n_p + -(-n_s // _LN_ROWS) * _LN_ROWS
        x1 = _res_ln(y_p, mixed_p, ln1_g[l], ln1_b[l], alpha, n_all, name="ln1_prompt")
        x1 = _res_ln(y_s, mixed_s, ln1_g[l], ln1_b[l], alpha, n_all, row0=n_p, into=x1, name="ln1_sample")

        w_r = jnp.zeros((d, _LANES), _F32).at[:, :n_groups].set(w_router_group[l]) \
                 .at[:, n_groups:n_groups + n_experts].set(w_router_expert[l])
        b_r = jnp.zeros((1, _LANES), _F32).at[0, :n_groups].set(b_router_group[l]) \
                 .at[0, n_groups:n_groups + n_experts].set(b_router_expert[l])
        slab_p = _router(x1, 0, n_p, w_r, b_r, n_groups, per_group, name="router_prompt")
        slab_s = _router(x1, n_p, n_s, w_r, b_r, n_groups, per_group, name="router_sample")
        expert = jnp.concatenate([slab_p[:, :TOP_K], slab_s[:, :TOP_K]], axis=0).astype(jnp.int32).reshape(-1)
        dest, row_src, block_e, n_active = _dispatch_plan(expert, n_experts, _MOE_ROWS)
        row_tok = jnp.maximum(row_src, 0) // TOP_K
        n_valid = jnp.sum((row_src >= 0).reshape(-1, _MOE_ROWS), axis=1).astype(jnp.int32)
        ys = _expert_ffn(x1, row_tok, n_valid, block_e, n_active, w1, w3, w2, l)
        y_p = _combine_ln(x1, 0, n_p, slab_p, dest[:n_p * TOP_K], ys, ln2_g[l], ln2_b[l], alpha,
                          name="combine_prompt")
        y_s = _combine_ln(x1, n_p, n_s, slab_s, dest[n_p * TOP_K:], ys, ln2_g[l], ln2_b[l], alpha,
                          name="combine_sample")

        outs["kp"].append(k_p.reshape(bsz, seq, n_heads, d_head))
        outs["vp"].append(v_p.reshape(bsz, seq, n_heads, d_head))
        outs["cp"].append(cs_p.reshape(bsz, taps - 1, c))
        outs["ks"].append(k_s.reshape(dec, dec_seq, n_heads, d_head))
        outs["vs"].append(v_s.reshape(dec, dec_seq, n_heads, d_head))
        outs["cs"].append(cs_s)
    return (y_p.reshape(bsz, seq, d), y_s.reshape(dec, dec_seq, d),
            jnp.stack(outs["kp"]), jnp.stack(outs["vp"]), jnp.stack(outs["cp"]),
            jnp.stack(outs["ks"]), jnp.stack(outs["vs"]), jnp.stack(outs["cs"]))
```
